```python
import math
import jax, jax.numpy as jnp
from jax import lax
import numpy as np

D_MODEL = 2048
BATCH = 2
SEQ = 8192
DEPTH = 1
DEC_BATCH = 32
DEC_SEQ = 4
PAST_LEN = 16384
PAGE_SIZE = 128

N_META = 16
SB_HEADS = 16
SB_HEAD_DIM = 128
SB_WIDTH = SB_HEADS * SB_HEAD_DIM
SB_SCALE = SB_HEAD_DIM ** -0.5
SB_BIAS_LO = -8.0
SB_BIAS_HI = -4.0
Q_BLOCK = 128
SSM_WIDTH = D_MODEL
SSM_HEAD_DIM = 64
SSM_HEADS = SSM_WIDTH // SSM_HEAD_DIM
SSM_GROUPS = 4
SSM_STATE = 128
CONV_WIDTH = 4
CONV_DIM = SSM_WIDTH + 2 * SSM_GROUPS * SSM_STATE
SSD_CHUNK = 128
DT_MIN = 0.001
DT_MAX = 0.1
D_FF = 5632
IN_DIM = 3 * SB_WIDTH + SSM_WIDTH + CONV_DIM + SSM_HEADS
DEEPNORM_ALPHA = (2.0 * DEPTH) ** 0.25
DEEPNORM_BETA = (8.0 * DEPTH) ** -0.25
LN_EPS = 1e-5
RMS_EPS = 1e-5

kernel_name = 'stickbreak_ssd_macaron_deepnorm_step'


def layer_norm(x, g, b):
    xf = x.astype(jnp.float32)
    mu = jnp.mean(xf, -1, keepdims=True)
    xc = xf - mu
    var = jnp.mean(xc * xc, -1, keepdims=True)
    return (xc * lax.rsqrt(var + LN_EPS) * g.astype(jnp.float32) + b.astype(jnp.float32)).astype(x.dtype)


def swiglu(x, w1, w3, w2):
    return (jax.nn.silu(x @ w1) * (x @ w3)) @ w2


def sb_weights(z, mask):
    log_keep = jnp.where(mask, jax.nn.log_sigmoid(-z), 0.0)
    suffix = lax.cumsum(log_keep, axis=z.ndim - 1, reverse=True)
    after = jnp.concatenate([suffix[..., 1:], jnp.zeros_like(suffix[..., :1])], axis=-1)
    return jnp.where(mask, jnp.exp(jax.nn.log_sigmoid(z) + after), 0.0)


def sb_block(q, q_pos, k, v, k_pos, bias):
    z = jnp.einsum('bqhd,bkhd->bhqk', q.astype(jnp.float32), k.astype(jnp.float32)) * SB_SCALE
    z = z + bias.astype(jnp.float32)[None, :, None, None]
    a = sb_weights(z, k_pos[None, :] < q_pos[:, None])
    return jnp.einsum('bhqk,bkhd->bqhd', a, v.astype(jnp.float32))


def sb_prompt(q, k, v, bias):
    b, t = q.shape[:2]
    pos = jnp.arange(t)
    o_meta = sb_block(q[:, :N_META], pos[:N_META], k[:, :N_META], v[:, :N_META], pos[:N_META], bias)
    n_blk = (t - N_META) // Q_BLOCK
    qb = jnp.swapaxes(q[:, N_META:].reshape(b, n_blk, Q_BLOCK, SB_HEADS, SB_HEAD_DIM), 0, 1)
    pb = pos[N_META:].reshape(n_blk, Q_BLOCK)
    o = lax.map(lambda xs: sb_block(xs[0], xs[1], k, v, pos, bias), (qb, pb))
    o = jnp.swapaxes(o, 0, 1).reshape(b, t - N_META, SB_HEADS, SB_HEAD_DIM)
    return jnp.concatenate([o_meta, o], axis=1)


def sb_sample(q, k_new, v_new, cache_k, cache_v, layer, page_table, bias):
    b, s = q.shape[:2]
    n_pages = page_table.shape[1]
    past = n_pages * PAGE_SIZE
    qf = q.astype(jnp.float32) * SB_SCALE

    def page_logits(pid):
        return jnp.einsum('bqhd,bphd->bhqp', qf, cache_k[layer, pid].astype(jnp.float32))

    z_past = lax.map(page_logits, page_table.T)
    z_past = jnp.moveaxis(z_past, 0, 3).reshape(b, SB_HEADS, s, past)
    z_new = jnp.einsum('bqhd,bkhd->bhqk', qf, k_new.astype(jnp.float32))
    z = jnp.concatenate([z_past, z_new], axis=-1)
    z = z + bias.astype(jnp.float32)[None, :, None, None]
    k_pos = jnp.arange(past + s)
    q_pos = past + jnp.arange(s)
    a = sb_weights(z, k_pos[None, :] < q_pos[:, None])
    a_past = jnp.moveaxis(a[..., :past].reshape(b, SB_HEADS, s, n_pages, PAGE_SIZE), 3, 0)

    def acc_page(acc, xs):
        pid, a_pg = xs
        return acc + jnp.einsum('bhqp,bphd->bqhd', a_pg, cache_v[layer, pid].astype(jnp.float32)), None

    o, _ = lax.scan(acc_page, jnp.zeros((b, s, SB_HEADS, SB_HEAD_DIM), jnp.float32), (page_table.T, a_past))
    return o + jnp.einsum('bhqk,bkhd->bqhd', a[..., past:], v_new.astype(jnp.float32))


def causal_dwconv(xpad, w, bias):
    t = xpad.shape[1] - CONV_WIDTH + 1
    out = bias
    for i in range(CONV_WIDTH):
        out = out + xpad[:, i:i + t] * w[i]
    return out


def ssd_scan(x, dt, a, bm, cm, h0, chunk):
    b, t = x.shape[:2]
    c = t // chunk
    r = SSM_HEADS // SSM_GROUPS
    x = x.reshape(b, c, chunk, SSM_GROUPS, r, SSM_HEAD_DIM)
    dt = dt.reshape(b, c, chunk, SSM_GROUPS, r)
    bm = bm.reshape(b, c, chunk, SSM_GROUPS, SSM_STATE)
    cm = cm.reshape(b, c, chunk, SSM_GROUPS, SSM_STATE)
    acs = jnp.cumsum(dt * a.reshape(SSM_GROUPS, r), axis=2)
    xdt = x * dt[..., None]
    causal = jnp.tril(jnp.ones((chunk, chunk), dtype=bool))
    seg = acs[:, :, :, None] - acs[:, :, None, :]
    decay = jnp.exp(jnp.where(causal[:, :, None, None], seg, -jnp.inf))
    cb = jnp.einsum('bclgn,bcsgn->bclsg', cm, bm)
    y_diag = jnp.einsum('bclsgr,bcsgrp->bclgrp', cb[..., None] * decay, xdt)
    to_end = jnp.exp(acs[:, :, -1:] - acs)
    st = jnp.einsum('bclgn,bclgrp->bcgrpn', bm, xdt * to_end[..., None])
    chunk_decay = jnp.exp(acs[:, :, -1])

    def step(h, xs):
        dec, s_c = xs
        return dec[..., None, None] * h + s_c, h

    h_last, h_in = lax.scan(step, h0.reshape(b, SSM_GROUPS, r, SSM_HEAD_DIM, SSM_STATE),
                            (jnp.moveaxis(chunk_decay, 1, 0), jnp.moveaxis(st, 1, 0)))
    y_off = jnp.einsum('bclgn,cbgrpn->bclgrp', cm, h_in) * jnp.exp(acs)[..., None]
    y = (y_diag + y_off).reshape(b, t, SSM_HEADS, SSM_HEAD_DIM)
    return y, h_last.reshape(b, SSM_HEADS, SSM_HEAD_DIM, SSM_STATE)


def gated_rmsnorm(y, z, w):
    g = y * jax.nn.silu(z.astype(jnp.float32))
    shp = g.shape
    g = g.reshape(shp[:-1] + (SSM_GROUPS, SSM_WIDTH // SSM_GROUPS))
    g = g * lax.rsqrt(jnp.mean(g * g, -1, keepdims=True) + RMS_EPS)
    return g.reshape(shp) * w.astype(jnp.float32)


def ssm_mix(xbc_conv, dtr, z, h0, front_pad, chunk, lw):
    xbc = jax.nn.silu(xbc_conv.astype(jnp.float32))
    xs, bm, cm = jnp.split(xbc, [SSM_WIDTH, SSM_WIDTH + SSM_GROUPS * SSM_STATE], axis=-1)
    b, t = xs.shape[:2]
    xs = xs.reshape(b, t, SSM_HEADS, SSM_HEAD_DIM)
    bm = bm.reshape(b, t, SSM_GROUPS, SSM_STATE)
    cm = cm.reshape(b, t, SSM_GROUPS, SSM_STATE)
    dt = jax.nn.softplus(dtr.astype(jnp.float32) + lw['dt_bias'].astype(jnp.float32))
    a = -jnp.exp(lw['a_log'].astype(jnp.float32))

    def pad(arr):
        return jnp.pad(arr, [(0, 0), (front_pad, 0)] + [(0, 0)] * (arr.ndim - 2))

    y, h_last = ssd_scan(pad(xs), pad(dt), a, pad(bm), pad(cm), h0.astype(jnp.float32), chunk)
    y = y[:, front_pad:] + xs * lw['d_skip'].astype(jnp.float32)[:, None]
    return gated_rmsnorm(y.reshape(b, t, SSM_WIDTH), z, lw['ssm_norm_w']), h_last


def split_in(u, w_in):
    b, t = u.shape[:2]
    p = u @ w_in
    cuts = [SB_WIDTH, 2 * SB_WIDTH, 3 * SB_WIDTH, 3 * SB_WIDTH + SSM_WIDTH, 3 * SB_WIDTH + SSM_WIDTH + CONV_DIM]
    q, k, v, z, xbc, dtr = jnp.split(p, cuts, axis=-1)
    heads = lambda arr: arr.reshape(b, t, SB_HEADS, SB_HEAD_DIM)
    return heads(q), heads(k), heads(v), z, xbc, dtr


def merge_out(u, o_attn, y_ssm, lw):
    b, t = u.shape[:2]
    gates = jax.nn.sigmoid((u @ lw['w_gate'] + lw['b_gate']).astype(jnp.float32))
    g_a, g_b = jnp.split(gates, 2, axis=-1)
    y_a = o_attn.reshape(b, t, SB_WIDTH).astype(u.dtype) @ lw['w_pa']
    y_b = y_ssm.astype(u.dtype) @ lw['w_pb']
    return (g_a * y_a + g_b * y_b).astype(u.dtype) @ lw['w_out']


def pre_mix(h, lw):
    f = swiglu(h, lw['ffn1_w1'], lw['ffn1_w3'], lw['ffn1_w2'])
    return layer_norm(DEEPNORM_ALPHA * h + 0.5 * f, lw['ln1_g'], lw['ln1_b'])


def post_mix(u, mix, lw):
    h = layer_norm(DEEPNORM_ALPHA * u + mix, lw['ln2_g'], lw['ln2_b'])
    f = swiglu(h, lw['ffn2_w1'], lw['ffn2_w3'], lw['ffn2_w2'])
    return layer_norm(DEEPNORM_ALPHA * h + 0.5 * f, lw['ln3_g'], lw['ln3_b'])


def setup_inputs(seed: int = 0) -> dict:
    key = jax.random.key(seed)
    keys = iter(jax.random.split(key, 48))
    f32 = jnp.float32
    n_pages = PAST_LEN // PAGE_SIZE
    n_phys = (DEC_BATCH * n_pages * 5) // 4

    def nrm(shape, scale):
        return jax.random.normal(next(keys), shape, f32) * scale

    def gain(width):
        return 1.0 + nrm((DEPTH, width), 0.02)

    x_prompt = jax.random.normal(next(keys), (BATCH, SEQ, D_MODEL), f32)
    x_sample = jax.random.normal(next(keys), (DEC_BATCH, DEC_SEQ, D_MODEL), f32)
    cache_k = jax.random.normal(next(keys), (DEPTH, n_phys, PAGE_SIZE, SB_HEADS, SB_HEAD_DIM), f32)
    cache_v = jax.random.normal(next(keys), (DEPTH, n_phys, PAGE_SIZE, SB_HEADS, SB_HEAD_DIM), f32)
    state_conv = jax.random.normal(next(keys), (DEPTH, DEC_BATCH, CONV_WIDTH - 1, CONV_DIM), f32)
    state_ssm = nrm((DEPTH, DEC_BATCH, SSM_HEADS, SSM_HEAD_DIM, SSM_STATE), 0.1)
    page_table = jax.random.permutation(next(keys), n_phys)[: DEC_BATCH * n_pages].reshape(DEC_BATCH, n_pages).astype(jnp.int32)
    meta_tokens = jax.random.normal(next(keys), (N_META, D_MODEL), f32)
    ln1_g = gain(D_MODEL)
    ln1_b = nrm((DEPTH, D_MODEL), 0.02)
    ffn1_w1 = nrm((DEPTH, D_MODEL, D_FF), D_MODEL ** -0.5)
    ffn1_w3 = nrm((DEPTH, D_MODEL, D_FF), D_MODEL ** -0.5)
    ffn1_w2 = nrm((DEPTH, D_FF, D_MODEL), DEEPNORM_BETA * D_FF ** -0.5)
    w_in = nrm((DEPTH, D_MODEL, IN_DIM), D_MODEL ** -0.5)
    sb_bias = jax.random.uniform(next(keys), (DEPTH, SB_HEADS), f32, SB_BIAS_LO, SB_BIAS_HI)
    conv_w = nrm((DEPTH, CONV_WIDTH, CONV_DIM), CONV_WIDTH ** -0.5)
    conv_b = nrm((DEPTH, CONV_DIM), 0.02)
    dt0 = jnp.exp(jax.random.uniform(next(keys), (DEPTH, SSM_HEADS), f32, math.log(DT_MIN), math.log(DT_MAX)))
    dt_bias = dt0 + jnp.log(-jnp.expm1(-dt0))
    a_log = jnp.log(jax.random.uniform(next(keys), (DEPTH, SSM_HEADS), f32, 1.0, 16.0))
    d_skip = gain(SSM_HEADS)
    ssm_norm_w = gain(SSM_WIDTH)
    w_gate = nrm((DEPTH, D_MODEL, 2 * D_MODEL), D_MODEL ** -0.5)
    b_gate = nrm((DEPTH, 2 * D_MODEL), 0.02)
    w_pa = nrm((DEPTH, SB_WIDTH, D_MODEL), DEEPNORM_BETA * SB_WIDTH ** -0.5)
    w_pb = nrm((DEPTH, SSM_WIDTH, D_MODEL), DEEPNORM_BETA * SSM_WIDTH ** -0.5)
    w_out = nrm((DEPTH, D_MODEL, D_MODEL), DEEPNORM_BETA * D_MODEL ** -0.5)
    ln2_g = gain(D_MODEL)
    ln2_b = nrm((DEPTH, D_MODEL), 0.02)
    ffn2_w1 = nrm((DEPTH, D_MODEL, D_FF), D_MODEL ** -0.5)
    ffn2_w3 = nrm((DEPTH, D_MODEL, D_FF), D_MODEL ** -0.5)
    ffn2_w2 = nrm((DEPTH, D_FF, D_MODEL), DEEPNORM_BETA * D_FF ** -0.5)
    ln3_g = gain(D_MODEL)
    ln3_b = nrm((DEPTH, D_MODEL), 0.02)
    return {'x_prompt': x_prompt, 'x_sample': x_sample, 'cache_k': cache_k, 'cache_v': cache_v,
            'state_conv': state_conv, 'state_ssm': state_ssm, 'page_table': page_table,
            'meta_tokens': meta_tokens, 'ln1_g': ln1_g, 'ln1_b': ln1_b,
            'ffn1_w1': ffn1_w1, 'ffn1_w3': ffn1_w3, 'ffn1_w2': ffn1_w2, 'w_in': w_in, 'sb_bias': sb_bias,
            'conv_w': conv_w, 'conv_b': conv_b, 'dt_bias': dt_bias, 'a_log': a_log, 'd_skip': d_skip,
            'ssm_norm_w': ssm_norm_w, 'w_gate': w_gate, 'b_gate': b_gate, 'w_pa': w_pa, 'w_pb': w_pb,
            'w_out': w_out, 'ln2_g': ln2_g, 'ln2_b': ln2_b,
            'ffn2_w1': ffn2_w1, 'ffn2_w3': ffn2_w3, 'ffn2_w2': ffn2_w2, 'ln3_g': ln3_g, 'ln3_b': ln3_b}


def reference(x_prompt, x_sample, cache_k, cache_v, state_conv, state_ssm, page_table, meta_tokens,
              ln1_g, ln1_b, ffn1_w1, ffn1_w3, ffn1_w2, w_in, sb_bias, conv_w, conv_b, dt_bias, a_log, d_skip,
              ssm_norm_w, w_gate, b_gate, w_pa, w_pb, w_out, ln2_g, ln2_b,
              ffn2_w1, ffn2_w3, ffn2_w2, ln3_g, ln3_b):
    b = x_prompt.shape[0]
    meta = jnp.broadcast_to(meta_tokens.astype(x_prompt.dtype)[None], (b, N_META, D_MODEL))
    hp = jnp.concatenate([meta, x_prompt], axis=1)
    hs = x_sample
    kp_l, vp_l, cp_l, sp_l, ks_l, vs_l, cs_l, ss_l = [], [], [], [], [], [], [], []
    for l in range(DEPTH):
        lw = {'ln1_g': ln1_g[l], 'ln1_b': ln1_b[l], 'ffn1_w1': ffn1_w1[l], 'ffn1_w3': ffn1_w3[l],
              'ffn1_w2': ffn1_w2[l], 'dt_bias': dt_bias[l], 'a_log': a_log[l], 'd_skip': d_skip[l],
              'ssm_norm_w': ssm_norm_w[l], 'w_gate': w_gate[l], 'b_gate': b_gate[l], 'w_pa': w_pa[l],
              'w_pb': w_pb[l], 'w_out': w_out[l], 'ln2_g': ln2_g[l], 'ln2_b': ln2_b[l],
              'ffn2_w1': ffn2_w1[l], 'ffn2_w3': ffn2_w3[l], 'ffn2_w2': ffn2_w2[l],
              'ln3_g': ln3_g[l], 'ln3_b': ln3_b[l]}
        cw, cbias = conv_w[l], conv_b[l]

        u = pre_mix(hp, lw)
        q, k, v, z, xbc, dtr = split_in(u, w_in[l])
        o_att = sb_prompt(q, k, v, sb_bias[l])
        xpad = jnp.pad(xbc, ((0, 0), (CONV_WIDTH - 1, 0), (0, 0)))
        h0 = jnp.zeros((b, SSM_HEADS, SSM_HEAD_DIM, SSM_STATE), jnp.float32)
        y_ssm, h_last = ssm_mix(causal_dwconv(xpad, cw, cbias), dtr, z, h0, SSD_CHUNK - N_META, SSD_CHUNK, lw)
        hp = post_mix(u, merge_out(u, o_att, y_ssm, lw), lw)
        kp_l.append(k)
        vp_l.append(v)
        cp_l.append(xbc[:, xbc.shape[1] - (CONV_WIDTH - 1):])
        sp_l.append(h_last)

        u = pre_mix(hs, lw)
        q, k, v, z, xbc, dtr = split_in(u, w_in[l])
        o_att = sb_sample(q, k, v, cache_k, cache_v, l, page_table, sb_bias[l])
        xpad = jnp.concatenate([state_conv[l].astype(xbc.dtype), xbc], axis=1)
        y_ssm, h_last = ssm_mix(causal_dwconv(xpad, cw, cbias), dtr, z, state_ssm[l], 0, xbc.shape[1], lw)
        hs = post_mix(u, merge_out(u, o_att, y_ssm, lw), lw)
        ks_l.append(k)
        vs_l.append(v)
        cs_l.append(xpad[:, xpad.shape[1] - (CONV_WIDTH - 1):])
        ss_l.append(h_last)

    y_prompt = hp[:, N_META:]
    y_sample = hs
    return (y_prompt, y_sample,
            jnp.stack(kp_l), jnp.stack(vp_l), jnp.stack(cp_l), jnp.stack(sp_l),
            jnp.stack(ks_l), jnp.stack(vs_l), jnp.stack(cs_l), jnp.stack(ss_l))
```

```python
import functools

import jax
import jax.numpy as jnp
from jax import lax
from jax.experimental import pallas as pl
from jax.experimental.pallas import tpu as pltpu

F32 = jnp.float32
BF16 = jnp.bfloat16
LN_EPS = 1e-5
RMS_EPS = 1e-5
LANES = 128
SUBLANES = 8
CONV_WIDTH = 4
VMEM_LIMIT = 56 * 1024 * 1024


def _cparams(sem):
    return pltpu.CompilerParams(dimension_semantics=sem, vmem_limit_bytes=VMEM_LIMIT)


def _pick(n, pref):
    if n <= pref:
        return n
    for step in (LANES, SUBLANES, 1):
        t = pref - pref % step
        while t >= step:
            if n % t == 0:
                return t
            t -= step
    return n


def _layer_norm(r, g, b):
    mu = jnp.mean(r, axis=-1, keepdims=True)
    xc = r - mu
    var = jnp.mean(xc * xc, axis=-1, keepdims=True)
    return xc * lax.rsqrt(var + LN_EPS) * g + b


def _softplus(z):
    return jnp.maximum(z, 0.0) + jnp.log(1.0 + jnp.exp(-jnp.abs(z)))


def _ffn_ln_kernel(x_ref, w1_ref, w3_ref, w2_ref, g_ref, b_ref, *rest, alpha):
    o_refs, (xb_ref, acc_ref) = rest[:-2], rest[-2:]
    f = pl.program_id(2)

    @pl.when(f == 0)
    def _():
        xb_ref[...] = x_ref[...].astype(BF16)
        acc_ref[...] = jnp.zeros_like(acc_ref)

    xb = xb_ref[...]
    a = jnp.dot(xb, w1_ref[...], preferred_element_type=F32)
    c = jnp.dot(xb, w3_ref[...], preferred_element_type=F32)
    hid = (a * jax.nn.sigmoid(a) * c).astype(BF16)
    acc_ref[...] += jnp.dot(hid, w2_ref[...], preferred_element_type=F32)

    @pl.when(f == pl.num_programs(2) - 1)
    def _():
        r = alpha * x_ref[...].astype(F32) + 0.5 * acc_ref[...]
        y = _layer_norm(r, g_ref[...], b_ref[...])
        for o_ref in o_refs:
            o_ref[...] = y.astype(o_ref.dtype)


def ffn_ln(x, w1, w3, w2, g, b, out_dtypes, *, alpha, tm=768, tf=256):
    bsz, t, d = x.shape
    dff = w1.shape[1]
    tm = _pick(t, tm)
    tf = _pick(dff, tf)
    grid = (bsz, t // tm, dff // tf)
    row = pl.BlockSpec((None, tm, d), lambda bi, i, f: (bi, i, 0))
    vec = pl.BlockSpec((1, d), lambda bi, i, f: (0, 0))
    return pl.pallas_call(
        functools.partial(_ffn_ln_kernel, alpha=alpha),
        grid=grid,
        in_specs=[row,
                  pl.BlockSpec((d, tf), lambda bi, i, f: (0, f)),
                  pl.BlockSpec((d, tf), lambda bi, i, f: (0, f)),
                  pl.BlockSpec((tf, d), lambda bi, i, f: (f, 0)),
                  vec, vec],
        out_specs=[row for _ in out_dtypes],
        out_shape=[jax.ShapeDtypeStruct((bsz, t, d), dt) for dt in out_dtypes],
        scratch_shapes=[pltpu.VMEM((tm, d), BF16), pltpu.VMEM((tm, d), F32)],
        compiler_params=_cparams(("parallel", "parallel", "arbitrary")),
        name="ffn_ln",
    )(x, w1, w3, w2, g, b)


def _mm_kernel(x_ref, w_ref, *o_refs, scale):
    acc = jnp.dot(x_ref[...], w_ref[...], preferred_element_type=F32)
    if scale != 1.0:
        acc = acc * scale
    for o_ref in o_refs:
        o_ref[...] = acc.astype(o_ref.dtype)


def mm(x, w, outs, *, scale=1.0, tm=768, tn=512):
    bsz, t, d = x.shape
    n = w.shape[1]
    tm = _pick(t, tm)
    tn = _pick(n, tn)
    grid = (bsz, t // tm, n // tn)
    oblk = pl.BlockSpec((None, tm, tn), lambda bi, i, j: (bi, i, j))
    return pl.pallas_call(
        functools.partial(_mm_kernel, scale=scale),
        grid=grid,
        in_specs=[pl.BlockSpec((None, tm, d), lambda bi, i, j: (bi, i, 0)),
                  pl.BlockSpec((d, tn), lambda bi, i, j: (0, j))],
        out_specs=[oblk for _ in outs],
        out_shape=[jax.ShapeDtypeStruct((bsz, rows, n), dt) for dt, rows in outs],
        compiler_params=_cparams(("parallel", "parallel", "arbitrary")),
        name="proj",
    )(x, w)


def _merge_kernel(u32_ref, u16_ref, oa_ref, ys_ref, wga_ref, wgb_ref, bga_ref, bgb_ref, wpa_ref, wpb_ref,
                  wout_ref, g_ref, b_ref, o32_ref, acc_ref, *, alpha):
    c = pl.program_id(2)

    @pl.when(c == 0)
    def _():
        acc_ref[...] = jnp.zeros_like(acc_ref)

    u = u16_ref[...]
    ga = jax.nn.sigmoid(jnp.dot(u, wga_ref[...], preferred_element_type=F32) + bga_ref[...])
    gb = jax.nn.sigmoid(jnp.dot(u, wgb_ref[...], preferred_element_type=F32) + bgb_ref[...])
    ya = jnp.dot(oa_ref[...], wpa_ref[...], preferred_element_type=F32)
    yb = jnp.dot(ys_ref[...], wpb_ref[...], preferred_element_type=F32)
    mix = (ga * ya + gb * yb).astype(BF16)
    acc_ref[...] += jnp.dot(mix, wout_ref[...], preferred_element_type=F32)

    @pl.when(c == pl.num_programs(2) - 1)
    def _():
        r = alpha * u32_ref[...] + acc_ref[...]
        o32_ref[...] = _layer_norm(r, g_ref[...], b_ref[...])


def merge_ln(u32, u16, oa, ys, w_gate, b_gate, w_pa, w_pb, w_out, g, b, *, alpha, tm=384, tc=512):
    bsz, t, d = u32.shape
    tm = _pick(t, tm)
    tc = _pick(d, tc)
    nc = d // tc
    grid = (bsz, t // tm, nc)
    row = pl.BlockSpec((None, tm, d), lambda bi, i, c: (bi, i, 0))
    vec = pl.BlockSpec((1, d), lambda bi, i, c: (0, 0))
    col = pl.BlockSpec((d, tc), lambda bi, i, c: (0, c))
    col_hi = pl.BlockSpec((d, tc), lambda bi, i, c: (0, c + nc))
    bias = pl.BlockSpec((1, tc), lambda bi, i, c: (0, c))
    bias_hi = pl.BlockSpec((1, tc), lambda bi, i, c: (0, c + nc))
    return pl.pallas_call(
        functools.partial(_merge_kernel, alpha=alpha),
        grid=grid,
        in_specs=[row, row, row, row, col, col_hi, bias, bias_hi, col, col,
                  pl.BlockSpec((tc, d), lambda bi, i, c: (c, 0)), vec, vec],
        out_specs=row,
        out_shape=jax.ShapeDtypeStruct((bsz, t, d), F32),
        scratch_shapes=[pltpu.VMEM((tm, d), F32)],
        compiler_params=_cparams(("parallel", "parallel", "arbitrary")),
        name="merge_ln",
    )(u32, u16, oa, ys, w_gate, w_gate, b_gate, b_gate, w_pa, w_pb, w_out, g, b)


def _neg_suffix_matrix(tk):
    r = lax.broadcasted_iota(jnp.int32, (tk, tk), 0)
    c = lax.broadcasted_iota(jnp.int32, (tk, tk), 1)
    return jnp.where(r > c, -1.0, 0.0).astype(BF16)


def _sb_tile(z, carry, uneg, mask):
    sp = _softplus(z)
    if mask is not None:
        sp = jnp.where(mask, sp, 0.0)
    after = jnp.dot(sp.astype(BF16), uneg, preferred_element_type=F32)
    a = jnp.exp(z - sp + after + carry)
    if mask is not None:
        a = jnp.where(mask, a, 0.0)
    return a, carry - jnp.sum(sp, axis=-1, keepdims=True)


def _sb_prompt_kernel(bias_ref, q_ref, k_ref, v_ref, o_ref, *, tq, tk):
    h = pl.program_id(1)
    qi = pl.program_id(2)
    q = q_ref[...]
    bias = bias_ref[h]
    uneg = _neg_suffix_matrix(tk)
    ratio = tq // tk

    def step(kblk, carry, acc, mask):
        start = pl.multiple_of(kblk * tk, tk)
        k = k_ref[pl.ds(start, tk), :]
        v = v_ref[pl.ds(start, tk), :]
        z = lax.dot_general(q, k, (((1,), (1,)), ((), ())), preferred_element_type=F32) + bias
        a, carry = _sb_tile(z, carry, uneg, mask)
        acc = acc + jnp.dot(a.astype(BF16), v, preferred_element_type=F32)
        return carry, acc

    carry = jnp.zeros((tq, 1), F32)
    acc = jnp.zeros((tq, q.shape[1]), F32)
    qpos = lax.broadcasted_iota(jnp.int32, (tq, tk), 0)
    kloc = lax.broadcasted_iota(jnp.int32, (tq, tk), 1)
    for dblk in reversed(range(ratio)):
        mask = (kloc + dblk * tk) < qpos
        carry, acc = step(qi * ratio + dblk, carry, acc, mask)

    def body(i, ca):
        return step(qi * ratio - 1 - i, ca[0], ca[1], None)

    carry, acc = lax.fori_loop(0, qi * ratio, body, (carry, acc))
    o_ref[...] = acc.astype(o_ref.dtype)


def sb_prompt(q, k, v, bias, *, heads, tq=256, tk=256):
    bsz, t, width = q.shape
    dh = width // heads
    tq = _pick(t, tq)
    tk = _pick(tq, tk)
    grid = (bsz, heads, t // tq)
    return pl.pallas_call(
        functools.partial(_sb_prompt_kernel, tq=tq, tk=tk),
        grid=grid,
        in_specs=[pl.BlockSpec(memory_space=pltpu.SMEM),
                  pl.BlockSpec((None, tq, dh), lambda bi, h, i: (bi, i, h)),
                  pl.BlockSpec((None, t, dh), lambda bi, h, i: (bi, 0, h)),
                  pl.BlockSpec((None, t, dh), lambda bi, h, i: (bi, 0, h))],
        out_specs=pl.BlockSpec((None, tq, dh), lambda bi, h, i: (bi, i, h)),
        out_shape=jax.ShapeDtypeStruct((bsz, t, width), BF16),
        compiler_params=_cparams(("parallel", "parallel", "arbitrary")),
        name="sb_prompt",
    )(bias, q, k, v)


def _sb_sample_kernel(pt_ref, qbd_ref, bias_ref, knew_ref, vnew_ref, *rest, pages_per_step, heads, dec_seq):
    kv_refs = rest[:2 * pages_per_step]
    o_ref, acc_ref, carry_ref = rest[2 * pages_per_step:]
    s = pl.program_id(1)
    npair = heads // 2
    page = knew_ref.shape[0] // heads
    rows = qbd_ref.shape[1]
    uneg = _neg_suffix_matrix(page)
    bias = bias_ref[...]

    def head_rows(ref, h):
        return ref[pl.ds(h, page, stride=heads), :]

    def do_page(k_ref, v_ref, mask):
        zs = []
        for p in range(npair):
            kp = jnp.concatenate([head_rows(k_ref, 2 * p), head_rows(k_ref, 2 * p + 1)], axis=1).astype(BF16)
            zs.append(lax.dot_general(qbd_ref[p].astype(BF16), kp, (((1,), (1,)), ((), ())),
                                      preferred_element_type=F32))
        z = jnp.concatenate(zs, axis=0) + bias
        a, carry = _sb_tile(z, carry_ref[...], uneg, mask)
        carry_ref[...] = carry
        for p in range(npair):
            vp = jnp.concatenate([head_rows(v_ref, 2 * p), head_rows(v_ref, 2 * p + 1)], axis=1).astype(BF16)
            acc_ref[p] += jnp.dot(a[p * rows:(p + 1) * rows].astype(BF16), vp, preferred_element_type=F32)

    @pl.when(s == 0)
    def _():
        acc_ref[...] = jnp.zeros_like(acc_ref)
        carry_ref[...] = jnp.zeros_like(carry_ref)
        qidx = lax.broadcasted_iota(jnp.int32, (npair * rows, page), 0) % dec_seq
        kidx = lax.broadcasted_iota(jnp.int32, (npair * rows, page), 1)
        do_page(knew_ref, vnew_ref, kidx < qidx)

    for p in range(pages_per_step):
        do_page(kv_refs[2 * p], kv_refs[2 * p + 1], None)

    @pl.when(s == pl.num_programs(1) - 1)
    def _():
        o_ref[...] = acc_ref[...]


def sb_sample(q, k_new, v_new, cache_k, cache_v, page_table, bias, *, pages_per_step=4):
    bsz, s, heads, dh = q.shape
    n_phys, page = cache_k.shape[:2]
    n_pages = page_table.shape[1]
    pps = _pick(n_pages, pages_per_step)
    npair = heads // 2
    rows = 2 * s
    qp = q.reshape(bsz, s, npair, 2, dh).transpose(0, 2, 3, 1, 4)
    eye = jnp.eye(2, dtype=q.dtype)
    qbd = (qp[:, :, :, :, None, :] * eye[None, None, :, None, :, None]).reshape(bsz, npair, rows, 2 * dh)
    qbd = qbd.astype(F32)
    bias_rows = jnp.broadcast_to(bias.astype(F32).reshape(npair, 2, 1, 1), (npair, 2, s, LANES)).reshape(npair * rows, LANES)
    pad = ((0, 0), (0, page - s), (0, 0), (0, 0))
    knew = jnp.pad(k_new, pad).reshape(bsz, page * heads, dh)
    vnew = jnp.pad(v_new, pad).reshape(bsz, page * heads, dh)
    ck = cache_k.reshape(n_phys, page * heads, dh)
    cv = cache_v.reshape(n_phys, page * heads, dh)

    def page_spec(p):
        return pl.BlockSpec((None, page * heads, dh),
                            lambda bi, st, pt: (pt[bi, n_pages - 1 - (st * pps + p)], 0, 0))

    kv_specs, kv_args = [], []
    for p in range(pps):
        kv_specs += [page_spec(p), page_spec(p)]
        kv_args += [ck, cv]
    new_spec = pl.BlockSpec((None, page * heads, dh), lambda bi, st, pt: (bi, 0, 0))
    out = pl.pallas_call(
        functools.partial(_sb_sample_kernel, pages_per_step=pps, heads=heads, dec_seq=s),
        grid_spec=pltpu.PrefetchScalarGridSpec(
            num_scalar_prefetch=1,
            grid=(bsz, n_pages // pps),
            in_specs=[pl.BlockSpec((None, npair, rows, 2 * dh), lambda bi, st, pt: (bi, 0, 0, 0)),
                      pl.BlockSpec((npair * rows, LANES), lambda bi, st, pt: (0, 0)),
                      new_spec, new_spec] + kv_specs,
            out_specs=pl.BlockSpec((None, npair, rows, 2 * dh), lambda bi, st, pt: (bi, 0, 0, 0)),
            scratch_shapes=[pltpu.VMEM((npair, rows, 2 * dh), F32), pltpu.VMEM((npair * rows, 1), F32)],
        ),
        out_shape=jax.ShapeDtypeStruct((bsz, npair, rows, 2 * dh), F32),
        compiler_params=_cparams(("parallel", "arbitrary")),
        name="sb_sample",
    )(page_table, qbd, bias_rows, knew, vnew, *kv_args)
    o = out.reshape(bsz, npair, 2, s, 2, dh)
    o = jnp.stack([o[:, :, 0, :, 0], o[:, :, 1, :, 1]], axis=2)
    return o.transpose(0, 3, 1, 2, 4).reshape(bsz, s, heads * dh)


def _split3(x):
    h1 = x.astype(BF16)
    r1 = x - h1.astype(F32)
    h2 = r1.astype(BF16)
    h3 = (r1 - h2.astype(F32)).astype(BF16)
    return h1, h2, h3


def _ssd_kernel(xs_ref, bm_ref, cm_ref, pxs_ref, pbm_ref, pcm_ref, wxs_ref, wbm_ref, wcm_ref,
                bxs_ref, bbm_ref, bcm_ref, dt_ref, dtb_ref, alog_ref, dskip_ref, z_ref, nw_ref, h0_ref,
                y_ref, hl_ref, state_ref, winx_ref, winb_ref, winc_ref, *, chunk, t_valid, hpg, hdim):
    c = pl.program_id(2)
    halo = SUBLANES

    @pl.when(c == 0)
    def _():
        state_ref[...] = h0_ref[...]
        winx_ref[0:halo, :] = pxs_ref[...]
        winb_ref[0:halo, :] = pbm_ref[...]
        winc_ref[0:halo, :] = pcm_ref[...]

    valid = (lax.broadcasted_iota(jnp.int32, (chunk, 1), 0) + c * chunk) < t_valid

    def conv_silu(win_ref, cur_ref, w_ref, b_ref):
        cur = cur_ref[...]
        win_ref[halo:halo + chunk, :] = cur
        out = b_ref[...]
        for i in range(CONV_WIDTH):
            lo = halo - (CONV_WIDTH - 1) + i
            out = out + win_ref[lo:lo + chunk, :] * w_ref[i:i + 1, :]
        win_ref[0:halo, :] = cur[chunk - halo:chunk, :]
        return jnp.where(valid, out * jax.nn.sigmoid(out), 0.0)

    xs = conv_silu(winx_ref, xs_ref, wxs_ref, bxs_ref)
    bm = conv_silu(winb_ref, bm_ref, wbm_ref, bbm_ref)
    cm = conv_silu(winc_ref, cm_ref, wcm_ref, bcm_ref)
    bm16 = bm.astype(BF16)
    cm16 = cm.astype(BF16)

    dt = jnp.where(valid, _softplus(dt_ref[...] + dtb_ref[...]), 0.0)
    da = dt * (-jnp.exp(alog_ref[...]))
    ri = lax.broadcasted_iota(jnp.int32, (chunk, chunk), 0)
    ci = lax.broadcasted_iota(jnp.int32, (chunk, chunk), 1)
    causal = ri >= ci
    tril = jnp.where(causal, 1.0, 0.0).astype(BF16)
    acs = sum(jnp.dot(tril, part, preferred_element_type=F32) for part in _split3(da))
    acs_t = acs.T
    cb = lax.dot_general(cm16, bm16, (((1,), (1,)), ((), ())), preferred_element_type=F32)

    ys = []
    for r in range(hpg):
        a_col = acs[:, r:r + 1]
        a_row = acs_t[r:r + 1, :]
        a_last = acs[chunk - 1:chunk, r:r + 1]
        decay = jnp.exp(jnp.where(causal, a_col - a_row, -jnp.inf))
        x_r = xs[:, r * hdim:(r + 1) * hdim]
        xdt = x_r * dt[:, r:r + 1]
        st = state_ref[r]
        y_diag = jnp.dot((cb * decay).astype(BF16), xdt.astype(BF16), preferred_element_type=F32)
        y_off = lax.dot_general(cm16, st.astype(BF16), (((1,), (1,)), ((), ())),
                                preferred_element_type=F32) * jnp.exp(a_col)
        upd = lax.dot_general((xdt * jnp.exp(a_last - a_col)).astype(BF16), bm16, (((0,), (0,)), ((), ())),
                              preferred_element_type=F32)
        state_ref[r] = jnp.exp(a_last) * st + upd
        ys.append(y_diag + y_off + x_r * dskip_ref[:, r:r + 1])
    y = jnp.concatenate(ys, axis=1)
    zz = z_ref[...].astype(F32)
    gated = y * (zz * jax.nn.sigmoid(zz))
    gated = gated * lax.rsqrt(jnp.mean(gated * gated, axis=-1, keepdims=True) + RMS_EPS)
    y_ref[...] = (gated * nw_ref[...]).astype(y_ref.dtype)

    @pl.when(c == pl.num_programs(2) - 1)
    def _():
        hl_ref[...] = state_ref[...]


def ssd_mix(xbc, prev, dt, z, h0, conv_w, conv_b, dt_bias, a_log, d_skip, norm_w, *, t_valid, groups, chunk=128):
    bsz, t, _ = xbc.shape
    heads, hdim, n = h0.shape[1:]
    width = heads * hdim
    hpg = heads // groups
    gw = width // groups
    chunk = _pick(t, chunk)
    nchunk = -(-t_valid // chunk)
    assert gw % LANES == 0 and n == LANES
    xoff, boff, coff = 0, width // n, width // n + groups

    def rows(w, off):
        return pl.BlockSpec((None, chunk, w), lambda bi, g, c: (bi, c, g + off) if w == n else (bi, c, g))

    def first(w, off):
        return pl.BlockSpec((None, SUBLANES, w), lambda bi, g, c: (bi, 0, g + off) if w == n else (bi, 0, g))

    def par(r, w, off):
        return pl.BlockSpec((r, w), lambda bi, g, c: (0, g + off) if w == n else (0, g))

    lane = pl.BlockSpec((1, LANES), lambda bi, g, c: (0, g))
    st_spec = pl.BlockSpec((None, hpg, hdim, n), lambda bi, g, c: (bi, g, 0, 0))
    return pl.pallas_call(
        functools.partial(_ssd_kernel, chunk=chunk, t_valid=t_valid, hpg=hpg, hdim=hdim),
        grid=(bsz, groups, nchunk),
        in_specs=[rows(gw, xoff), rows(n, boff), rows(n, coff),
                  first(gw, xoff), first(n, boff), first(n, coff),
                  par(CONV_WIDTH, gw, xoff), par(CONV_WIDTH, n, boff), par(CONV_WIDTH, n, coff),
                  par(1, gw, xoff), par(1, n, boff), par(1, n, coff),
                  pl.BlockSpec((None, chunk, LANES), lambda bi, g, c: (bi, c, g)),
                  lane, lane, lane,
                  pl.BlockSpec((None, chunk, gw), lambda bi, g, c: (bi, c, g)),
                  pl.BlockSpec((1, gw), lambda bi, g, c: (0, g)),
                  st_spec],
        out_specs=[pl.BlockSpec((None, chunk, gw), lambda bi, g, c: (bi, c, g)), st_spec],
        out_shape=[jax.ShapeDtypeStruct((bsz, t, width), BF16), jax.ShapeDtypeStruct((bsz, heads, hdim, n), F32)],
        scratch_shapes=[pltpu.VMEM((hpg, hdim, n), F32),
                        pltpu.VMEM((chunk + SUBLANES, gw), F32),
                        pltpu.VMEM((chunk + SUBLANES, n), F32),
                        pltpu.VMEM((chunk + SUBLANES, n), F32)],
        compiler_params=_cparams(("parallel", "parallel", "arbitrary")),
        name="ssd_mix",
    )(xbc, xbc, xbc, prev, prev, prev, conv_w, conv_w, conv_w, conv_b, conv_b, conv_b,
      dt, dt_bias, a_log, d_skip, z, norm_w, h0)


def _group_lanes(v, groups):
    hpg = v.shape[0] // groups
    return jnp.pad(v.astype(F32).reshape(groups, hpg), ((0, 0), (0, LANES - hpg))).reshape(1, groups * LANES)


def kernel(x_prompt, x_sample, cache_k, cache_v, state_conv, state_ssm, page_table, meta_tokens,
           ln1_g, ln1_b, ffn1_w1, ffn1_w3, ffn1_w2, w_in, sb_bias, conv_w, conv_b, dt_bias, a_log, d_skip,
           ssm_norm_w, w_gate, b_gate, w_pa, w_pb, w_out, ln2_g, ln2_b,
           ffn2_w1, ffn2_w3, ffn2_w2, ln3_g, ln3_b):
    depth = w_in.shape[0]
    assert depth == 1
    bsz, seq, d = x_prompt.shape
    dec_b, dec_s, _ = x_sample.shape
    n_meta = meta_tokens.shape[0]
    sb_heads, sb_dh = cache_k.shape[3:]
    sb_width = sb_heads * sb_dh
    page = cache_k.shape[2]
    ssm_heads, ssm_hdim, ssm_n = state_ssm.shape[2:]
    ssm_width = ssm_heads * ssm_hdim
    conv_dim = conv_w.shape[2]
    groups = (conv_dim - ssm_width) // (2 * ssm_n)
    hpg = ssm_heads // groups
    alpha = (2.0 * depth) ** 0.25
    scale = sb_dh ** -0.5
    t_real = seq + n_meta
    row_tile = 256
    t_pad = -(-t_real // row_tile) * row_tile
    chunk = 128

    l = 0
    bf = lambda w: w.astype(BF16)
    vec = lambda v: v.astype(F32).reshape(1, -1)
    w1a, w3a, w2a = bf(ffn1_w1[l]), bf(ffn1_w3[l]), bf(ffn1_w2[l])
    w1b, w3b, w2b = bf(ffn2_w1[l]), bf(ffn2_w3[l]), bf(ffn2_w2[l])
    cuts = [0, sb_width, 2 * sb_width, 3 * sb_width, 3 * sb_width + ssm_width, 3 * sb_width + ssm_width + conv_dim]
    wq, wk, wv, wz, wx = (bf(w_in[l][:, cuts[i]:cuts[i + 1]]) for i in range(5))
    wdt = w_in[l][:, cuts[5]:].reshape(d, groups, hpg)
    wdt = bf(jnp.pad(wdt, ((0, 0), (0, 0), (0, LANES - hpg))).reshape(d, groups * LANES))
    wg, wpa, wpb, wo = bf(w_gate[l]), bf(w_pa[l]), bf(w_pb[l]), bf(w_out[l])
    dtb_l, alog_l, dskip_l = (_group_lanes(v[l], groups) for v in (dt_bias, a_log, d_skip))
    cw, cb = conv_w[l].astype(F32), vec(conv_b[l])
    bias = sb_bias[l].astype(F32)

    def pre(h):
        return ffn_ln(h, w1a, w3a, w2a, vec(ln1_g[l]), vec(ln1_b[l]), [F32, BF16], alpha=alpha)

    def project(u16, t_out):
        q, = mm(u16, wq, [(BF16, u16.shape[1])], scale=scale)
        k32, k16 = mm(u16, wk, [(F32, t_out), (BF16, u16.shape[1])])
        v32, v16 = mm(u16, wv, [(F32, t_out), (BF16, u16.shape[1])])
        z16, = mm(u16, wz, [(BF16, u16.shape[1])])
        xbc, = mm(u16, wx, [(F32, u16.shape[1])])
        dt, = mm(u16, wdt, [(F32, u16.shape[1])])
        return q, k32, k16, v32, v16, z16, xbc, dt

    def post(u32, u16, o_att, y_ssm):
        h32 = merge_ln(u32, u16, o_att, y_ssm, wg, vec(b_gate[l]), wpa, wpb, wo,
                       vec(ln2_g[l]), vec(ln2_b[l]), alpha=alpha)
        out, = ffn_ln(h32, w1b, w3b, w2b, vec(ln3_g[l]), vec(ln3_b[l]), [F32], alpha=alpha)
        return out

    def ssm(xbc, prev, dt, z16, h0, t_valid):
        return ssd_mix(xbc, prev, dt, z16, h0, cw, cb, dtb_l, alog_l, dskip_l, vec(ssm_norm_w[l]),
                       t_valid=t_valid, groups=groups, chunk=chunk)

    meta = jnp.broadcast_to(meta_tokens.astype(x_prompt.dtype)[None], (bsz, n_meta, d))
    hp = jnp.concatenate([meta, x_prompt, jnp.zeros((bsz, t_pad - t_real, d), x_prompt.dtype)], axis=1)
    u32, u16 = pre(hp)
    q, k32, k16, v32, v16, z16, xbc, dt = project(u16, t_real)
    o_att = sb_prompt(q, k16, v16, bias, heads=sb_heads)
    y_ssm, h_last_p = ssm(xbc, jnp.zeros((bsz, SUBLANES, conv_dim), F32), dt, z16,
                          jnp.zeros((bsz, ssm_heads, ssm_hdim, ssm_n), F32), t_real)
    y_prompt = post(u32, u16, o_att, y_ssm)[:, n_meta:t_real]
    k_prompt = k32.reshape(1, bsz, t_real, sb_heads, sb_dh)
    v_prompt = v32.reshape(1, bsz, t_real, sb_heads, sb_dh)
    conv_prompt = xbc[:, t_real - (CONV_WIDTH - 1):t_real][None]
    ssm_prompt = h_last_p[None]

    rows = dec_b * dec_s
    u32, u16 = pre(x_sample.reshape(1, rows, d))
    q, k32, _, v32, _, z16, xbc, dt = project(u16, rows)
    k_new = k32.reshape(dec_b, dec_s, sb_heads, sb_dh)
    v_new = v32.reshape(dec_b, dec_s, sb_heads, sb_dh)
    o_att = sb_sample(q.reshape(dec_b, dec_s, sb_heads, sb_dh), k_new, v_new, cache_k[l], cache_v[l], page_table, bias)
    per_seq = lambda a: jnp.pad(a.reshape(dec_b, dec_s, a.shape[-1]), ((0, 0), (0, chunk - dec_s), (0, 0)))
    xbc_s = xbc.reshape(dec_b, dec_s, conv_dim)
    prev = jnp.pad(state_conv[l].astype(F32), ((0, 0), (SUBLANES - (CONV_WIDTH - 1), 0), (0, 0)))
    y_ssm, h_last_s = ssm(per_seq(xbc), prev, per_seq(dt), per_seq(z16), state_ssm[l].astype(F32), dec_s)
    y_ssm = y_ssm[:, :dec_s].reshape(1, rows, ssm_width)
    y_sample = post(u32, u16, o_att.astype(BF16).reshape(1, rows, sb_width), y_ssm).reshape(dec_b, dec_s, d)
    tail = jnp.concatenate([state_conv[l].astype(F32), xbc_s], axis=1)
    conv_sample = tail[:, tail.shape[1] - (CONV_WIDTH - 1):][None]

    return (y_prompt, y_sample, k_prompt, v_prompt, conv_prompt, ssm_prompt,
            k_new[None], v_new[None], conv_sample, h_last_s[None])
```

```python
import functools

import jax
import jax.numpy as jnp
from jax import lax
from jax.experimental import pallas as pl
from jax.experimental.pallas import tpu as pltpu

F32 = jnp.float32
BF16 = jnp.bfloat16
LN_EPS = 1e-5
RMS_EPS = 1e-5
LANES = 128
SUBLANES = 8
CONV_WIDTH = 4
VMEM_LIMIT = 56 * 1024 * 1024


def _cparams(sem):
    return pltpu.CompilerParams(dimension_semantics=sem, vmem_limit_bytes=VMEM_LIMIT)


def _pick(n, pref):
    if n <= pref:
        return n
    for step in (LANES, SUBLANES, 1):
        t = pref - pref % step
        while t >= step:
            if n % t == 0:
                return t
            t -= step
    return n


def _layer_norm(r, g, b):
    mu = jnp.mean(r, axis=-1, keepdims=True)
    xc = r - mu
    var = jnp.mean(xc * xc, axis=-1, keepdims=True)
    return xc * lax.rsqrt(var + LN_EPS) * g + b


def _softplus(z):
    return jnp.maximum(z, 0.0) + jnp.log(1.0 + jnp.exp(-jnp.abs(z)))


def _ffn_ln_kernel(x_ref, w1_ref, w3_ref, w2_ref, g_ref, b_ref, *rest, alpha):
    o_refs, (xb_ref, acc_ref) = rest[:-2], rest[-2:]
    f = pl.program_id(2)

    @pl.when(f == 0)
    def _():
        xb_ref[...] = x_ref[...].astype(BF16)
        acc_ref[...] = jnp.zeros_like(acc_ref)

    xb = xb_ref[...]
    a = jnp.dot(xb, w1_ref[...], preferred_element_type=F32)
    c = jnp.dot(xb, w3_ref[...], preferred_element_type=F32)
    hid = (a * jax.nn.sigmoid(a) * c).astype(BF16)
    acc_ref[...] += jnp.dot(hid, w2_ref[...], preferred_element_type=F32)

    @pl.when(f == pl.num_programs(2) - 1)
    def _():
        r = alpha * x_ref[...].astype(F32) + 0.5 * acc_ref[...]
        y = _layer_norm(r, g_ref[...], b_ref[...])
        for o_ref in o_refs:
            o_ref[...] = y.astype(o_ref.dtype)


def ffn_ln(x, w1, w3, w2, g, b, out_dtypes, *, alpha, tm=768, tf=256):
    bsz, t, d = x.shape
    dff = w1.shape[1]
    tm = _pick(t, tm)
    tf = _pick(dff, tf)
    grid = (bsz, t // tm, dff // tf)
    row = pl.BlockSpec((None, tm, d), lambda bi, i, f: (bi, i, 0))
    vec = pl.BlockSpec((1, d), lambda bi, i, f: (0, 0))
    return pl.pallas_call(
        functools.partial(_ffn_ln_kernel, alpha=alpha),
        grid=grid,
        in_specs=[row,
                  pl.BlockSpec((d, tf), lambda bi, i, f: (0, f)),
                  pl.BlockSpec((d, tf), lambda bi, i, f: (0, f)),
                  pl.BlockSpec((tf, d), lambda bi, i, f: (f, 0)),
                  vec, vec],
        out_specs=[row for _ in out_dtypes],
        out_shape=[jax.ShapeDtypeStruct((bsz, t, d), dt) for dt in out_dtypes],
        scratch_shapes=[pltpu.VMEM((tm, d), BF16), pltpu.VMEM((tm, d), F32)],
        compiler_params=_cparams(("parallel", "parallel", "arbitrary")),
        name="ffn_ln",
    )(x, w1, w3, w2, g, b)


def _mm_kernel(x_ref, w_ref, *o_refs, scale):
    acc = jnp.dot(x_ref[...], w_ref[...], preferred_element_type=F32)
    if scale != 1.0:
        acc = acc * scale
    for o_ref in o_refs:
        o_ref[...] = acc.astype(o_ref.dtype)


def mm(x, w, outs, *, scale=1.0, tm=768, tn=512):
    bsz, t, d = x.shape
    n = w.shape[1]
    tm = _pick(t, tm)
    tn = _pick(n, tn)
    grid = (bsz, t // tm, n // tn)
    oblk = pl.BlockSpec((None, tm, tn), lambda bi, i, j: (bi, i, j))
    return pl.pallas_call(
        functools.partial(_mm_kernel, scale=scale),
        grid=grid,
        in_specs=[pl.BlockSpec((None, tm, d), lambda bi, i, j: (bi, i, 0)),
                  pl.BlockSpec((d, tn), lambda bi, i, j: (0, j))],
        out_specs=[oblk for _ in outs],
        out_shape=[jax.ShapeDtypeStruct((bsz, rows, n), dt) for dt, rows in outs],
        compiler_params=_cparams(("parallel", "parallel", "arbitrary")),
        name="proj",
    )(x, w)


def _merge_kernel(u32_ref, u16_ref, oa_ref, ys_ref, wga_ref, wgb_ref, bga_ref, bgb_ref, wpa_ref, wpb_ref,
                  wout_ref, g_ref, b_ref, o32_ref, acc_ref, *, alpha):
    c = pl.program_id(2)

    @pl.when(c == 0)
    def _():
        acc_ref[...] = jnp.zeros_like(acc_ref)

    u = u16_ref[...]
    ga = jax.nn.sigmoid(jnp.dot(u, wga_ref[...], preferred_element_type=F32) + bga_ref[...])
    gb = jax.nn.sigmoid(jnp.dot(u, wgb_ref[...], preferred_element_type=F32) + bgb_ref[...])
    ya = jnp.dot(oa_ref[...], wpa_ref[...], preferred_element_type=F32)
    yb = jnp.dot(ys_ref[...], wpb_ref[...], preferred_element_type=F32)
    mix = (ga * ya + gb * yb).astype(BF16)
    acc_ref[...] += jnp.dot(mix, wout_ref[...], preferred_element_type=F32)

    @pl.when(c == pl.num_programs(2) - 1)
    def _():
        r = alpha * u32_ref[...] + acc_ref[...]
        o32_ref[...] = _layer_norm(r, g_ref[...], b_ref[...])


def merge_ln(u32, u16, oa, ys, w_gate, b_gate, w_pa, w_pb, w_out, g, b, *, alpha, tm=384, tc=512):
    bsz, t, d = u32.shape
    tm = _pick(t, tm)
    tc = _pick(d, tc)
    nc = d // tc
    grid = (bsz, t // tm, nc)
    row = pl.BlockSpec((None, tm, d), lambda bi, i, c: (bi, i, 0))
    vec = pl.BlockSpec((1, d), lambda bi, i, c: (0, 0))
    col = pl.BlockSpec((d, tc), lambda bi, i, c: (0, c))
    col_hi = pl.BlockSpec((d, tc), lambda bi, i, c: (0, c + nc))
    bias = pl.BlockSpec((1, tc), lambda bi, i, c: (0, c))
    bias_hi = pl.BlockSpec((1, tc), lambda bi, i, c: (0, c + nc))
    return pl.pallas_call(
        functools.partial(_merge_kernel, alpha=alpha),
        grid=grid,
        in_specs=[row, row, row, row, col, col_hi, bias, bias_hi, col, col,
                  pl.BlockSpec((tc, d), lambda bi, i, c: (c, 0)), vec, vec],
        out_specs=row,
        out_shape=jax.ShapeDtypeStruct((bsz, t, d), F32),
        scratch_shapes=[pltpu.VMEM((tm, d), F32)],
        compiler_params=_cparams(("parallel", "parallel", "arbitrary")),
        name="merge_ln",
    )(u32, u16, oa, ys, w_gate, w_gate, b_gate, b_gate, w_pa, w_pb, w_out, g, b)


def _neg_suffix_matrix(tk):
    r = lax.broadcasted_iota(jnp.int32, (tk, tk), 0)
    c = lax.broadcasted_iota(jnp.int32, (tk, tk), 1)
    return jnp.where(r > c, -1.0, 0.0).astype(BF16)


LOG2E = 1.4426950408889634


def _sb_tile(z2, carry, uneg, mask):
    sp = jnp.maximum(z2, 0.0) + jnp.log(1.0 + jnp.exp2(-jnp.abs(z2))) * LOG2E
    if mask is not None:
        sp = jnp.where(mask, sp, 0.0)
    after = jnp.dot(sp.astype(BF16), uneg, preferred_element_type=F32)
    a = jnp.exp2(z2 - sp + after + carry)
    if mask is not None:
        a = jnp.where(mask, a, 0.0)
    return a, carry - jnp.sum(sp, axis=-1, keepdims=True)


def _sb_prompt_kernel(bias_ref, q_ref, k_ref, v_ref, o_ref, acc_ref, *, tq, tk, hps, dh):
    hg = pl.program_id(1)
    qi = pl.program_id(2)
    uneg = _neg_suffix_matrix(tk)
    ratio = tq // tk
    heads = range(hps)
    qs = [q_ref[:, j * dh:(j + 1) * dh] for j in heads]
    biases = [bias_ref[hg * hps + j] for j in heads]

    def logits(kblk):
        start = pl.multiple_of(kblk * tk, tk)
        return tuple(lax.dot_general(qs[j], k_ref[pl.ds(start, tk), j * dh:(j + 1) * dh], (((1,), (1,)), ((), ())),
                                     preferred_element_type=F32) + biases[j] for j in heads)

    def weigh(kblk, ws):
        start = pl.multiple_of(kblk * tk, tk)
        for j in heads:
            acc_ref[:, j * dh:(j + 1) * dh] += jnp.dot(ws[j], v_ref[pl.ds(start, tk), j * dh:(j + 1) * dh],
                                                       preferred_element_type=F32)

    def tiles(zs, carries, mask):
        out = [_sb_tile(zs[j], carries[j], uneg, mask) for j in heads]
        return tuple(a.astype(BF16) for a, _ in out), tuple(c for _, c in out)

    carries = tuple(jnp.zeros((tq, 1), F32) for _ in heads)
    acc_ref[...] = jnp.zeros_like(acc_ref)
    qpos = lax.broadcasted_iota(jnp.int32, (tq, tk), 0)
    kloc = lax.broadcasted_iota(jnp.int32, (tq, tk), 1)
    first = qi * ratio
    zs = logits(first + ratio - 1)
    ws = None
    for dblk in reversed(range(ratio)):
        zs_next = logits(jnp.maximum(first + dblk - 1, 0))
        if ws is not None:
            weigh(first + dblk + 1, ws)
        ws, carries = tiles(zs, carries, (kloc + dblk * tk) < qpos)
        zs = zs_next

    def body(i, state):
        zs, ws, carries = state
        kblk = first - 1 - i
        zs_next = logits(jnp.maximum(kblk - 1, 0))
        weigh(kblk + 1, ws)
        ws, carries = tiles(zs, carries, None)
        return zs_next, ws, carries

    _, ws, _ = lax.fori_loop(0, first, body, (zs, ws, carries))
    weigh(0, ws)
    o_ref[...] = acc_ref[...].astype(o_ref.dtype)


def sb_prompt(q, k, v, bias2, *, heads, tq=256, tk=256, heads_per_step=1):
    bsz, t, width = q.shape
    dh = width // heads
    tq = _pick(t, tq)
    tk = _pick(tq, tk)
    hps = _pick(heads, heads_per_step)
    grid = (bsz, heads // hps, t // tq)
    return pl.pallas_call(
        functools.partial(_sb_prompt_kernel, tq=tq, tk=tk, hps=hps, dh=dh),
        grid=grid,
        in_specs=[pl.BlockSpec(memory_space=pltpu.SMEM),
                  pl.BlockSpec((None, tq, hps * dh), lambda bi, h, i: (bi, i, h)),
                  pl.BlockSpec((None, t, hps * dh), lambda bi, h, i: (bi, 0, h)),
                  pl.BlockSpec((None, t, hps * dh), lambda bi, h, i: (bi, 0, h))],
        out_specs=pl.BlockSpec((None, tq, hps * dh), lambda bi, h, i: (bi, i, h)),
        out_shape=jax.ShapeDtypeStruct((bsz, t, width), BF16),
        scratch_shapes=[pltpu.VMEM((tq, hps * dh), F32)],
        compiler_params=_cparams(("parallel", "parallel", "arbitrary")),
        name="sb_prompt",
    )(bias2, q, k, v)


def _sb_sample_kernel(pt_ref, qbd_ref, bias_ref, knew_ref, vnew_ref, *rest, pages_per_step, heads, dec_seq):
    kv_refs = rest[:2 * pages_per_step]
    o_ref, acc_ref, carry_ref = rest[2 * pages_per_step:]
    s = pl.program_id(1)
    npair = heads // 2
    page = knew_ref.shape[0] // heads
    rows = qbd_ref.shape[1]
    uneg = _neg_suffix_matrix(page)
    bias = bias_ref[...]

    def head_rows(ref, h):
        return ref[pl.ds(h, page, stride=heads), :]

    def do_page(k_ref, v_ref, mask):
        zs = []
        for p in range(npair):
            kp = jnp.concatenate([head_rows(k_ref, 2 * p), head_rows(k_ref, 2 * p + 1)], axis=1).astype(BF16)
            zs.append(lax.dot_general(qbd_ref[p].astype(BF16), kp, (((1,), (1,)), ((), ())),
                                      preferred_element_type=F32))
        z = jnp.concatenate(zs, axis=0) + bias
        a, carry = _sb_tile(z, carry_ref[...], uneg, mask)
        carry_ref[...] = carry
        for p in range(npair):
            vp = jnp.concatenate([head_rows(v_ref, 2 * p), head_rows(v_ref, 2 * p + 1)], axis=1).astype(BF16)
            acc_ref[p] += jnp.dot(a[p * rows:(p + 1) * rows].astype(BF16), vp, preferred_element_type=F32)

    @pl.when(s == 0)
    def _():
        acc_ref[...] = jnp.zeros_like(acc_ref)
        carry_ref[...] = jnp.zeros_like(carry_ref)
        qidx = lax.broadcasted_iota(jnp.int32, (npair * rows, page), 0) % dec_seq
        kidx = lax.broadcasted_iota(jnp.int32, (npair * rows, page), 1)
        do_page(knew_ref, vnew_ref, kidx < qidx)

    for p in range(pages_per_step):
        do_page(kv_refs[2 * p], kv_refs[2 * p + 1], None)

    @pl.when(s == pl.num_programs(1) - 1)
    def _():
        o_ref[...] = acc_ref[...]


def sb_sample(q, k_new, v_new, cache_k, cache_v, page_table, bias, *, pages_per_step=4):
    bsz, s, heads, dh = q.shape
    n_phys, page = cache_k.shape[:2]
    n_pages = page_table.shape[1]
    pps = _pick(n_pages, pages_per_step)
    npair = heads // 2
    rows = 2 * s
    qp = q.reshape(bsz, s, npair, 2, dh).transpose(0, 2, 3, 1, 4)
    eye = jnp.eye(2, dtype=q.dtype)
    qbd = (qp[:, :, :, :, None, :] * eye[None, None, :, None, :, None]).reshape(bsz, npair, rows, 2 * dh)
    qbd = qbd.astype(F32)
    bias_rows = jnp.broadcast_to(bias.astype(F32).reshape(npair, 2, 1, 1), (npair, 2, s, LANES)).reshape(npair * rows, LANES)
    pad = ((0, 0), (0, page - s), (0, 0), (0, 0))
    knew = jnp.pad(k_new, pad).reshape(bsz, page * heads, dh)
    vnew = jnp.pad(v_new, pad).reshape(bsz, page * heads, dh)
    ck = cache_k.reshape(n_phys, page * heads, dh)
    cv = cache_v.reshape(n_phys, page * heads, dh)

    def page_spec(p):
        return pl.BlockSpec((None, page * heads, dh),
                            lambda bi, st, pt: (pt[bi, n_pages - 1 - (st * pps + p)], 0, 0))

    kv_specs, kv_args = [], []
    for p in range(pps):
        kv_specs += [page_spec(p), page_spec(p)]
        kv_args += [ck, cv]
    new_spec = pl.BlockSpec((None, page * heads, dh), lambda bi, st, pt: (bi, 0, 0))
    out = pl.pallas_call(
        functools.partial(_sb_sample_kernel, pages_per_step=pps, heads=heads, dec_seq=s),
        grid_spec=pltpu.PrefetchScalarGridSpec(
            num_scalar_prefetch=1,
            grid=(bsz, n_pages // pps),
            in_specs=[pl.BlockSpec((None, npair, rows, 2 * dh), lambda bi, st, pt: (bi, 0, 0, 0)),
                      pl.BlockSpec((npair * rows, LANES), lambda bi, st, pt: (0, 0)),
                      new_spec, new_spec] + kv_specs,
            out_specs=pl.BlockSpec((None, npair, rows, 2 * dh), lambda bi, st, pt: (bi, 0, 0, 0)),
            scratch_shapes=[pltpu.VMEM((npair, rows, 2 * dh), F32), pltpu.VMEM((npair * rows, 1), F32)],
        ),
        out_shape=jax.ShapeDtypeStruct((bsz, npair, rows, 2 * dh), F32),
        compiler_params=_cparams(("parallel", "arbitrary")),
        name="sb_sample",
    )(page_table, qbd, bias_rows, knew, vnew, *kv_args)
    o = out.reshape(bsz, npair, 2, s, 2, dh)
    o = jnp.stack([o[:, :, 0, :, 0], o[:, :, 1, :, 1]], axis=2)
    return o.transpose(0, 3, 1, 2, 4).reshape(bsz, s, heads * dh)


def _split3(x):
    h1 = x.astype(BF16)
    r1 = x - h1.astype(F32)
    h2 = r1.astype(BF16)
    h3 = (r1 - h2.astype(F32)).astype(BF16)
    return h1, h2, h3


def _ssd_kernel(xs_ref, bm_ref, cm_ref, pxs_ref, pbm_ref, pcm_ref, wxs_ref, wbm_ref, wcm_ref,
                bxs_ref, bbm_ref, bcm_ref, dt_ref, dtb_ref, alog_ref, dskip_ref, z_ref, nw_ref, h0_ref,
                y_ref, hl_ref, state_ref, winx_ref, winb_ref, winc_ref, *, chunk, t_valid, hpg, hdim):
    c = pl.program_id(2)
    halo = SUBLANES

    @pl.when(c == 0)
    def _():
        state_ref[...] = h0_ref[...]
        winx_ref[0:halo, :] = pxs_ref[...]
        winb_ref[0:halo, :] = pbm_ref[...]
        winc_ref[0:halo, :] = pcm_ref[...]

    valid = (lax.broadcasted_iota(jnp.int32, (chunk, 1), 0) + c * chunk) < t_valid

    def conv_silu(win_ref, cur_ref, w_ref, b_ref):
        cur = cur_ref[...]
        win_ref[halo:halo + chunk, :] = cur
        out = b_ref[...]
        for i in range(CONV_WIDTH):
            lo = halo - (CONV_WIDTH - 1) + i
            out = out + win_ref[lo:lo + chunk, :] * w_ref[i:i + 1, :]
        win_ref[0:halo, :] = cur[chunk - halo:chunk, :]
        return jnp.where(valid, out * jax.nn.sigmoid(out), 0.0)

    xs = conv_silu(winx_ref, xs_ref, wxs_ref, bxs_ref)
    bm = conv_silu(winb_ref, bm_ref, wbm_ref, bbm_ref)
    cm = conv_silu(winc_ref, cm_ref, wcm_ref, bcm_ref)
    bm16 = bm.astype(BF16)
    cm16 = cm.astype(BF16)

    dt = jnp.where(valid, _softplus(dt_ref[...] + dtb_ref[...]), 0.0)
    da = dt * (-jnp.exp(alog_ref[...]))
    ri = lax.broadcasted_iota(jnp.int32, (chunk, chunk), 0)
    ci = lax.broadcasted_iota(jnp.int32, (chunk, chunk), 1)
    causal = ri >= ci
    tril = jnp.where(causal, 1.0, 0.0).astype(BF16)
    acs = sum(jnp.dot(tril, part, preferred_element_type=F32) for part in _split3(da))
    acs_t = acs.T
    cb = lax.dot_general(cm16, bm16, (((1,), (1,)), ((), ())), preferred_element_type=F32)

    ys = []
    for r in range(hpg):
        a_col = acs[:, r:r + 1]
        a_row = acs_t[r:r + 1, :]
        a_last = acs[chunk - 1:chunk, r:r + 1]
        decay = jnp.exp(jnp.where(causal, a_col - a_row, -jnp.inf))
        x_r = xs[:, r * hdim:(r + 1) * hdim]
        xdt = x_r * dt[:, r:r + 1]
        st = state_ref[r]
        y_diag = jnp.dot((cb * decay).astype(BF16), xdt.astype(BF16), preferred_element_type=F32)
        y_off = lax.dot_general(cm16, st.astype(BF16), (((1,), (1,)), ((), ())),
                                preferred_element_type=F32) * jnp.exp(a_col)
        upd = lax.dot_general((xdt * jnp.exp(a_last - a_col)).astype(BF16), bm16, (((0,), (0,)), ((), ())),
                              preferred_element_type=F32)
        state_ref[r] = jnp.exp(a_last) * st + upd
        ys.append(y_diag + y_off + x_r * dskip_ref[:, r:r + 1])
    y = jnp.concatenate(ys, axis=1)
    zz = z_ref[...].astype(F32)
    gated = y * (zz * jax.nn.sigmoid(zz))
    gated = gated * lax.rsqrt(jnp.mean(gated * gated, axis=-1, keepdims=True) + RMS_EPS)
    y_ref[...] = (gated * nw_ref[...]).astype(y_ref.dtype)

    @pl.when(c == pl.num_programs(2) - 1)
    def _():
        hl_ref[...] = state_ref[...]


def ssd_mix(xbc, prev, dt, z, h0, conv_w, conv_b, dt_bias, a_log, d_skip, norm_w, *, t_valid, groups, chunk=128):
    bsz, t, _ = xbc.shape
    heads, hdim, n = h0.shape[1:]
    width = heads * hdim
    hpg = heads // groups
    gw = width // groups
    chunk = _pick(t, chunk)
    nchunk = -(-t_valid // chunk)
    assert gw % LANES == 0 and n == LANES
    xoff, boff, coff = 0, width // n, width // n + groups

    def rows(w, off):
        return pl.BlockSpec((None, chunk, w), lambda bi, g, c: (bi, c, g + off) if w == n else (bi, c, g))

    def first(w, off):
        return pl.BlockSpec((None, SUBLANES, w), lambda bi, g, c: (bi, 0, g + off) if w == n else (bi, 0, g))

    def par(r, w, off):
        return pl.BlockSpec((r, w), lambda bi, g, c: (0, g + off) if w == n else (0, g))

    lane = pl.BlockSpec((1, LANES), lambda bi, g, c: (0, g))
    st_spec = pl.BlockSpec((None, hpg, hdim, n), lambda bi, g, c: (bi, g, 0, 0))
    return pl.pallas_call(
        functools.partial(_ssd_kernel, chunk=chunk, t_valid=t_valid, hpg=hpg, hdim=hdim),
        grid=(bsz, groups, nchunk),
        in_specs=[rows(gw, xoff), rows(n, boff), rows(n, coff),
                  first(gw, xoff), first(n, boff), first(n, coff),
                  par(CONV_WIDTH, gw, xoff), par(CONV_WIDTH, n, boff), par(CONV_WIDTH, n, coff),
                  par(1, gw, xoff), par(1, n, boff), par(1, n, coff),
                  pl.BlockSpec((None, chunk, LANES), lambda bi, g, c: (bi, c, g)),
                  lane, lane, lane,
                  pl.BlockSpec((None, chunk, gw), lambda bi, g, c: (bi, c, g)),
                  pl.BlockSpec((1, gw), lambda bi, g, c: (0, g)),
                  st_spec],
        out_specs=[pl.BlockSpec((None, chunk, gw), lambda bi, g, c: (bi, c, g)), st_spec],
        out_shape=[jax.ShapeDtypeStruct((bsz, t, width), BF16), jax.ShapeDtypeStruct((bsz, heads, hdim, n), F32)],
        scratch_shapes=[pltpu.VMEM((hpg, hdim, n), F32),
                        pltpu.VMEM((chunk + SUBLANES, gw), F32),
                        pltpu.VMEM((chunk + SUBLANES, n), F32),
                        pltpu.VMEM((chunk + SUBLANES, n), F32)],
        compiler_params=_cparams(("parallel", "parallel", "arbitrary")),
        name="ssd_mix",
    )(xbc, xbc, xbc, prev, prev, prev, conv_w, conv_w, conv_w, conv_b, conv_b, conv_b,
      dt, dt_bias, a_log, d_skip, z, norm_w, h0)


def _group_lanes(v, groups):
    hpg = v.shape[0] // groups
    return jnp.pad(v.astype(F32).reshape(groups, hpg), ((0, 0), (0, LANES - hpg))).reshape(1, groups * LANES)


def kernel(x_prompt, x_sample, cache_k, cache_v, state_conv, state_ssm, page_table, meta_tokens,
           ln1_g, ln1_b, ffn1_w1, ffn1_w3, ffn1_w2, w_in, sb_bias, conv_w, conv_b, dt_bias, a_log, d_skip,
           ssm_norm_w, w_gate, b_gate, w_pa, w_pb, w_out, ln2_g, ln2_b,
           ffn2_w1, ffn2_w3, ffn2_w2, ln3_g, ln3_b):
    depth = w_in.shape[0]
    assert depth == 1
    bsz, seq, d = x_prompt.shape
    dec_b, dec_s, _ = x_sample.shape
    n_meta = meta_tokens.shape[0]
    sb_heads, sb_dh = cache_k.shape[3:]
    sb_width = sb_heads * sb_dh
    page = cache_k.shape[2]
    ssm_heads, ssm_hdim, ssm_n = state_ssm.shape[2:]
    ssm_width = ssm_heads * ssm_hdim
    conv_dim = conv_w.shape[2]
    groups = (conv_dim - ssm_width) // (2 * ssm_n)
    hpg = ssm_heads // groups
    alpha = (2.0 * depth) ** 0.25
    scale = sb_dh ** -0.5 * LOG2E
    t_real = seq + n_meta
    row_tile = 256
    t_pad = -(-t_real // row_tile) * row_tile
    chunk = 128

    l = 0
    bf = lambda w: w.astype(BF16)
    vec = lambda v: v.astype(F32).reshape(1, -1)
    w1a, w3a, w2a = bf(ffn1_w1[l]), bf(ffn1_w3[l]), bf(ffn1_w2[l])
    w1b, w3b, w2b = bf(ffn2_w1[l]), bf(ffn2_w3[l]), bf(ffn2_w2[l])
    cuts = [0, sb_width, 2 * sb_width, 3 * sb_width, 3 * sb_width + ssm_width, 3 * sb_width + ssm_width + conv_dim]
    wq, wk, wv, wz, wx = (bf(w_in[l][:, cuts[i]:cuts[i + 1]]) for i in range(5))
    wdt = w_in[l][:, cuts[5]:].reshape(d, groups, hpg)
    wdt = bf(jnp.pad(wdt, ((0, 0), (0, 0), (0, LANES - hpg))).reshape(d, groups * LANES))
    wg, wpa, wpb, wo = bf(w_gate[l]), bf(w_pa[l]), bf(w_pb[l]), bf(w_out[l])
    dtb_l, alog_l, dskip_l = (_group_lanes(v[l], groups) for v in (dt_bias, a_log, d_skip))
    cw, cb = conv_w[l].astype(F32), vec(conv_b[l])
    bias = sb_bias[l].astype(F32) * LOG2E

    def pre(h):
        return ffn_ln(h, w1a, w3a, w2a, vec(ln1_g[l]), vec(ln1_b[l]), [F32, BF16], alpha=alpha)

    def project(u16, t_out):
        q, = mm(u16, wq, [(BF16, u16.shape[1])], scale=scale)
        k32, k16 = mm(u16, wk, [(F32, t_out), (BF16, u16.shape[1])])
        v32, v16 = mm(u16, wv, [(F32, t_out), (BF16, u16.shape[1])])
        z16, = mm(u16, wz, [(BF16, u16.shape[1])])
        xbc, = mm(u16, wx, [(F32, u16.shape[1])])
        dt, = mm(u16, wdt, [(F32, u16.shape[1])])
        return q, k32, k16, v32, v16, z16, xbc, dt

    def post(u32, u16, o_att, y_ssm):
        h32 = merge_ln(u32, u16, o_att, y_ssm, wg, vec(b_gate[l]), wpa, wpb, wo,
                       vec(ln2_g[l]), vec(ln2_b[l]), alpha=alpha)
        out, = ffn_ln(h32, w1b, w3b, w2b, vec(ln3_g[l]), vec(ln3_b[l]), [F32], alpha=alpha)
        return out

    def ssm(xbc, prev, dt, z16, h0, t_valid):
        return ssd_mix(xbc, prev, dt, z16, h0, cw, cb, dtb_l, alog_l, dskip_l, vec(ssm_norm_w[l]),
                       t_valid=t_valid, groups=groups, chunk=chunk)

    meta = jnp.broadcast_to(meta_tokens.astype(x_prompt.dtype)[None], (bsz, n_meta, d))
    hp = jnp.concatenate([meta, x_prompt, jnp.zeros((bsz, t_pad - t_real, d), x_prompt.dtype)], axis=1)
    u32, u16 = pre(hp)
    q, k32, k16, v32, v16, z16, xbc, dt = project(u16, t_real)
    o_att = sb_prompt(q, k16, v16, bias, heads=sb_heads)
    y_ssm, h_last_p = ssm(xbc, jnp.zeros((bsz, SUBLANES, conv_dim), F32), dt, z16,
                          jnp.zeros((bsz, ssm_heads, ssm_hdim, ssm_n), F32), t_real)
    y_prompt = post(u32, u16, o_att, y_ssm)[:, n_meta:t_real]
    k_prompt = k32.reshape(1, bsz, t_real, sb_heads, sb_dh)
    v_prompt = v32.reshape(1, bsz, t_real, sb_heads, sb_dh)
    conv_prompt = xbc[:, t_real - (CONV_WIDTH - 1):t_real][None]
    ssm_prompt = h_last_p[None]

    rows = dec_b * dec_s
    u32, u16 = pre(x_sample.reshape(1, rows, d))
    q, k32, _, v32, _, z16, xbc, dt = project(u16, rows)
    k_new = k32.reshape(dec_b, dec_s, sb_heads, sb_dh)
    v_new = v32.reshape(dec_b, dec_s, sb_heads, sb_dh)
    o_att = sb_sample(q.reshape(dec_b, dec_s, sb_heads, sb_dh), k_new, v_new, cache_k[l], cache_v[l], page_table, bias)
    per_seq = lambda a: jnp.pad(a.reshape(dec_b, dec_s, a.shape[-1]), ((0, 0), (0, chunk - dec_s), (0, 0)))
    xbc_s = xbc.reshape(dec_b, dec_s, conv_dim)
    prev = jnp.pad(state_conv[l].astype(F32), ((0, 0), (SUBLANES - (CONV_WIDTH - 1), 0), (0, 0)))
    y_ssm, h_last_s = ssm(per_seq(xbc), prev, per_seq(dt), per_seq(z16), state_ssm[l].astype(F32), dec_s)
    y_ssm = y_ssm[:, :dec_s].reshape(1, rows, ssm_width)
    y_sample = post(u32, u16, o_att.astype(BF16).reshape(1, rows, sb_width), y_ssm).reshape(dec_b, dec_s, d)
    tail = jnp.concatenate([state_conv[l].astype(F32), xbc_s], axis=1)
    conv_sample = tail[:, tail.shape[1] - (CONV_WIDTH - 1):][None]

    return (y_prompt, y_sample, k_prompt, v_prompt, conv_prompt, ssm_prompt,
            k_new[None], v_new[None], conv_sample, h_last_s[None])
```

```python
import functools

import jax
import jax.numpy as jnp
from jax import lax
from jax.experimental import pallas as pl
from jax.experimental.pallas import tpu as pltpu

F32 = jnp.float32
BF16 = jnp.bfloat16
LN_EPS = 1e-5
RMS_EPS = 1e-5
LANES = 128
SUBLANES = 8
CONV_WIDTH = 4
VMEM_LIMIT = 56 * 1024 * 1024


def _cparams(sem):
    return pltpu.CompilerParams(dimension_semantics=sem, vmem_limit_bytes=VMEM_LIMIT)


def _pick(n, pref):
    if n <= pref:
        return n
    for step in (LANES, SUBLANES, 1):
        t = pref - pref % step
        while t >= step:
            if n % t == 0:
                return t
            t -= step
    return n


def _layer_norm(r, g, b):
    mu = jnp.mean(r, axis=-1, keepdims=True)
    xc = r - mu
    var = jnp.mean(xc * xc, axis=-1, keepdims=True)
    return xc * lax.rsqrt(var + LN_EPS) * g + b


def _softplus(z):
    return jnp.maximum(z, 0.0) + jnp.log(1.0 + jnp.exp(-jnp.abs(z)))


def _ffn_ln_kernel(x_ref, w1_ref, w3_ref, w2_ref, g_ref, b_ref, *rest, alpha):
    o_refs, (xb_ref, acc_ref) = rest[:-2], rest[-2:]
    f = pl.program_id(2)

    @pl.when(f == 0)
    def _():
        xb_ref[...] = x_ref[...].astype(BF16)
        acc_ref[...] = jnp.zeros_like(acc_ref)

    xb = xb_ref[...]
    a = jnp.dot(xb, w1_ref[...], preferred_element_type=F32)
    c = jnp.dot(xb, w3_ref[...], preferred_element_type=F32)
    hid = (a * jax.nn.sigmoid(a) * c).astype(BF16)
    acc_ref[...] += jnp.dot(hid, w2_ref[...], preferred_element_type=F32)

    @pl.when(f == pl.num_programs(2) - 1)
    def _():
        r = alpha * x_ref[...].astype(F32) + 0.5 * acc_ref[...]
        y = _layer_norm(r, g_ref[...], b_ref[...])
        for o_ref in o_refs:
            o_ref[...] = y.astype(o_ref.dtype)


def ffn_ln(x, w1, w3, w2, g, b, out_dtypes, *, alpha, tm=768, tf=256):
    bsz, t, d = x.shape
    dff = w1.shape[1]
    tm = _pick(t, tm)
    tf = _pick(dff, tf)
    grid = (bsz, t // tm, dff // tf)
    row = pl.BlockSpec((None, tm, d), lambda bi, i, f: (bi, i, 0))
    vec = pl.BlockSpec((1, d), lambda bi, i, f: (0, 0))
    return pl.pallas_call(
        functools.partial(_ffn_ln_kernel, alpha=alpha),
        grid=grid,
        in_specs=[row,
                  pl.BlockSpec((d, tf), lambda bi, i, f: (0, f)),
                  pl.BlockSpec((d, tf), lambda bi, i, f: (0, f)),
                  pl.BlockSpec((tf, d), lambda bi, i, f: (f, 0)),
                  vec, vec],
        out_specs=[row for _ in out_dtypes],
        out_shape=[jax.ShapeDtypeStruct((bsz, t, d), dt) for dt in out_dtypes],
        scratch_shapes=[pltpu.VMEM((tm, d), BF16), pltpu.VMEM((tm, d), F32)],
        compiler_params=_cparams(("parallel", "parallel", "arbitrary")),
        name="ffn_ln",
    )(x, w1, w3, w2, g, b)


def _mm_kernel(x_ref, w_ref, *o_refs, scale):
    acc = jnp.dot(x_ref[...], w_ref[...], preferred_element_type=F32)
    if scale != 1.0:
        acc = acc * scale
    for o_ref in o_refs:
        o_ref[...] = acc.astype(o_ref.dtype)


def mm(x, w, outs, *, scale=1.0, tm=1408, tn=512):
    bsz, t, d = x.shape
    n = w.shape[1]
    tm = _pick(t, tm)
    tn = _pick(n, tn)
    grid = (bsz, t // tm, n // tn)
    oblk = pl.BlockSpec((None, tm, tn), lambda bi, i, j: (bi, i, j))
    return pl.pallas_call(
        functools.partial(_mm_kernel, scale=scale),
        grid=grid,
        in_specs=[pl.BlockSpec((None, tm, d), lambda bi, i, j: (bi, i, 0)),
                  pl.BlockSpec((d, tn), lambda bi, i, j: (0, j))],
        out_specs=[oblk for _ in outs],
        out_shape=[jax.ShapeDtypeStruct((bsz, rows, n), dt) for dt, rows in outs],
        compiler_params=_cparams(("parallel", "parallel", "arbitrary")),
        name="proj",
    )(x, w)


def _merge_kernel(u32_ref, u16_ref, oa_ref, ys_ref, wga_ref, wgb_ref, bga_ref, bgb_ref, wpa_ref, wpb_ref,
                  wout_ref, g_ref, b_ref, o32_ref, acc_ref, *, alpha):
    c = pl.program_id(2)

    @pl.when(c == 0)
    def _():
        acc_ref[...] = jnp.zeros_like(acc_ref)

    u = u16_ref[...]
    ga = jax.nn.sigmoid(jnp.dot(u, wga_ref[...], preferred_element_type=F32) + bga_ref[...])
    gb = jax.nn.sigmoid(jnp.dot(u, wgb_ref[...], preferred_element_type=F32) + bgb_ref[...])
    ya = jnp.dot(oa_ref[...], wpa_ref[...], preferred_element_type=F32)
    yb = jnp.dot(ys_ref[...], wpb_ref[...], preferred_element_type=F32)
    mix = (ga * ya + gb * yb).astype(BF16)
    acc_ref[...] += jnp.dot(mix, wout_ref[...], preferred_element_type=F32)

    @pl.when(c == pl.num_programs(2) - 1)
    def _():
        r = alpha * u32_ref[...] + acc_ref[...]
        o32_ref[...] = _layer_norm(r, g_ref[...], b_ref[...])


def merge_ln(u32, u16, oa, ys, w_gate, b_gate, w_pa, w_pb, w_out, g, b, *, alpha, tm=384, tc=512):
    bsz, t, d = u32.shape
    tm = _pick(t, tm)
    tc = _pick(d, tc)
    nc = d // tc
    grid = (bsz, t // tm, nc)
    row = pl.BlockSpec((None, tm, d), lambda bi, i, c: (bi, i, 0))
    vec = pl.BlockSpec((1, d), lambda bi, i, c: (0, 0))
    col = pl.BlockSpec((d, tc), lambda bi, i, c: (0, c))
    col_hi = pl.BlockSpec((d, tc), lambda bi, i, c: (0, c + nc))
    bias = pl.BlockSpec((1, tc), lambda bi, i, c: (0, c))
    bias_hi = pl.BlockSpec((1, tc), lambda bi, i, c: (0, c + nc))
    return pl.pallas_call(
        functools.partial(_merge_kernel, alpha=alpha),
        grid=grid,
        in_specs=[row, row, row, row, col, col_hi, bias, bias_hi, col, col,
                  pl.BlockSpec((tc, d), lambda bi, i, c: (c, 0)), vec, vec],
        out_specs=row,
        out_shape=jax.ShapeDtypeStruct((bsz, t, d), F32),
        scratch_shapes=[pltpu.VMEM((tm, d), F32)],
        compiler_params=_cparams(("parallel", "parallel", "arbitrary")),
        name="merge_ln",
    )(u32, u16, oa, ys, w_gate, w_gate, b_gate, b_gate, w_pa, w_pb, w_out, g, b)


def _neg_suffix_matrix(tk):
    r = lax.broadcasted_iota(jnp.int32, (tk, tk), 0)
    c = lax.broadcasted_iota(jnp.int32, (tk, tk), 1)
    return jnp.where(r > c, -1.0, 0.0).astype(BF16)


LOG2E = 1.4426950408889634


def _sb_tile(z2, carry, uneg, mask):
    sp = jnp.maximum(z2, 0.0) + jnp.log(1.0 + jnp.exp2(-jnp.abs(z2))) * LOG2E
    if mask is not None:
        sp = jnp.where(mask, sp, 0.0)
    after = jnp.dot(sp.astype(BF16), uneg, preferred_element_type=F32)
    a = jnp.exp2(z2 - sp + after + carry)
    if mask is not None:
        a = jnp.where(mask, a, 0.0)
    return a, carry - jnp.sum(sp, axis=-1, keepdims=True)


def _sb_prompt_kernel(bias_ref, q_ref, k_ref, v_ref, o_ref, acc_ref, z_ref, w_ref, *, tq, tk):
    qi = pl.program_id(2)
    q = q_ref[...]
    bias = bias_ref[pl.program_id(1)]
    uneg = _neg_suffix_matrix(tk)
    ratio = tq // tk

    def put_logits(kblk):
        start = pl.multiple_of(kblk * tk, tk)
        z_ref[kblk % 2] = lax.dot_general(q, k_ref[pl.ds(start, tk), :], (((1,), (1,)), ((), ())),
                                          preferred_element_type=F32) + bias

    def weigh(kblk):
        start = pl.multiple_of(kblk * tk, tk)
        acc_ref[...] += jnp.dot(w_ref[kblk % 2], v_ref[pl.ds(start, tk), :], preferred_element_type=F32)

    def sub_step(kblk, carry, mask, weigh_previous):
        slot = kblk % 2
        z = z_ref[slot]
        put_logits(jnp.maximum(kblk - 1, 0))
        if weigh_previous:
            weigh(kblk + 1)
        a, carry = _sb_tile(z, carry, uneg, mask)
        w_ref[slot] = a.astype(BF16)
        return carry

    acc_ref[...] = jnp.zeros_like(acc_ref)
    qpos = lax.broadcasted_iota(jnp.int32, (tq, tk), 0)
    kloc = lax.broadcasted_iota(jnp.int32, (tq, tk), 1)
    first = qi * ratio
    carry = jnp.zeros((tq, 1), F32)
    put_logits(first + ratio - 1)
    for dblk in reversed(range(ratio)):
        carry = sub_step(first + dblk, carry, (kloc + dblk * tk) < qpos, dblk != ratio - 1)
    lax.fori_loop(0, first, lambda i, c: sub_step(first - 1 - i, c, None, True), carry)
    weigh(0)
    o_ref[...] = acc_ref[...].astype(o_ref.dtype)


def sb_prompt(q, k, v, bias2, *, heads, tq=256, tk=256):
    bsz, t, width = q.shape
    dh = width // heads
    tq = _pick(t, tq)
    tk = _pick(tq, tk)
    return pl.pallas_call(
        functools.partial(_sb_prompt_kernel, tq=tq, tk=tk),
        grid=(bsz, heads, t // tq),
        in_specs=[pl.BlockSpec(memory_space=pltpu.SMEM),
                  pl.BlockSpec((None, tq, dh), lambda bi, h, i: (bi, i, h)),
                  pl.BlockSpec((None, t, dh), lambda bi, h, i: (bi, 0, h)),
                  pl.BlockSpec((None, t, dh), lambda bi, h, i: (bi, 0, h))],
        out_specs=pl.BlockSpec((None, tq, dh), lambda bi, h, i: (bi, i, h)),
        out_shape=jax.ShapeDtypeStruct((bsz, t, width), BF16),
        scratch_shapes=[pltpu.VMEM((tq, dh), F32), pltpu.VMEM((2, tq, tk), F32), pltpu.VMEM((2, tq, tk), BF16)],
        compiler_params=_cparams(("parallel", "parallel", "arbitrary")),
        name="sb_prompt",
    )(bias2, q, k, v)


def _sb_sample_kernel(pt_ref, qbd_ref, bias_ref, knew_ref, vnew_ref, *rest, pages_per_step, heads, dec_seq):
    kv_refs = rest[:2 * pages_per_step]
    o_ref, acc_ref, carry_ref = rest[2 * pages_per_step:]
    s = pl.program_id(1)
    npair = heads // 2
    page = knew_ref.shape[0] // heads
    rows = qbd_ref.shape[1]
    uneg = _neg_suffix_matrix(page)
    bias = bias_ref[...]

    def head_rows(ref, h):
        return ref[pl.ds(h, page, stride=heads), :]

    def do_page(k_ref, v_ref, mask):
        zs = []
        for p in range(npair):
            kp = jnp.concatenate([head_rows(k_ref, 2 * p), head_rows(k_ref, 2 * p + 1)], axis=1).astype(BF16)
            zs.append(lax.dot_general(qbd_ref[p].astype(BF16), kp, (((1,), (1,)), ((), ())),
                                      preferred_element_type=F32))
        z = jnp.concatenate(zs, axis=0) + bias
        a, carry = _sb_tile(z, carry_ref[...], uneg, mask)
        carry_ref[...] = carry
        for p in range(npair):
            vp = jnp.concatenate([head_rows(v_ref, 2 * p), head_rows(v_ref, 2 * p + 1)], axis=1).astype(BF16)
            acc_ref[p] += jnp.dot(a[p * rows:(p + 1) * rows].astype(BF16), vp, preferred_element_type=F32)

    @pl.when(s == 0)
    def _():
        acc_ref[...] = jnp.zeros_like(acc_ref)
        carry_ref[...] = jnp.zeros_like(carry_ref)
        qidx = lax.broadcasted_iota(jnp.int32, (npair * rows, page), 0) % dec_seq
        kidx = lax.broadcasted_iota(jnp.int32, (npair * rows, page), 1)
        do_page(knew_ref, vnew_ref, kidx < qidx)

    for p in range(pages_per_step):
        do_page(kv_refs[2 * p], kv_refs[2 * p + 1], None)

    @pl.when(s == pl.num_programs(1) - 1)
    def _():
        o_ref[...] = acc_ref[...]


def sb_sample(q, k_new, v_new, cache_k, cache_v, page_table, bias, *, pages_per_step=8):
    bsz, s, heads, dh = q.shape
    n_phys, page = cache_k.shape[:2]
    n_pages = page_table.shape[1]
    pps = _pick(n_pages, pages_per_step)
    npair = heads // 2
    rows = 2 * s
    qp = q.reshape(bsz, s, npair, 2, dh).transpose(0, 2, 3, 1, 4)
    eye = jnp.eye(2, dtype=q.dtype)
    qbd = (qp[:, :, :, :, None, :] * eye[None, None, :, None, :, None]).reshape(bsz, npair, rows, 2 * dh)
    qbd = qbd.astype(F32)
    bias_rows = jnp.broadcast_to(bias.astype(F32).reshape(npair, 2, 1, 1), (npair, 2, s, LANES)).reshape(npair * rows, LANES)
    pad = ((0, 0), (0, page - s), (0, 0), (0, 0))
    knew = jnp.pad(k_new, pad).reshape(bsz, page * heads, dh)
    vnew = jnp.pad(v_new, pad).reshape(bsz, page * heads, dh)
    ck = cache_k.reshape(n_phys, page * heads, dh)
    cv = cache_v.reshape(n_phys, page * heads, dh)

    def page_spec(p):
        return pl.BlockSpec((None, page * heads, dh),
                            lambda bi, st, pt: (pt[bi, n_pages - 1 - (st * pps + p)], 0, 0))

    kv_specs, kv_args = [], []
    for p in range(pps):
        kv_specs += [page_spec(p), page_spec(p)]
        kv_args += [ck, cv]
    new_spec = pl.BlockSpec((None, page * heads, dh), lambda bi, st, pt: (bi, 0, 0))
    out = pl.pallas_call(
        functools.partial(_sb_sample_kernel, pages_per_step=pps, heads=heads, dec_seq=s),
        grid_spec=pltpu.PrefetchScalarGridSpec(
            num_scalar_prefetch=1,
            grid=(bsz, n_pages // pps),
            in_specs=[pl.BlockSpec((None, npair, rows, 2 * dh), lambda bi, st, pt: (bi, 0, 0, 0)),
                      pl.BlockSpec((npair * rows, LANES), lambda bi, st, pt: (0, 0)),
                      new_spec, new_spec] + kv_specs,
            out_specs=pl.BlockSpec((None, npair, rows, 2 * dh), lambda bi, st, pt: (bi, 0, 0, 0)),
            scratch_shapes=[pltpu.VMEM((npair, rows, 2 * dh), F32), pltpu.VMEM((npair * rows, 1), F32)],
        ),
        out_shape=jax.ShapeDtypeStruct((bsz, npair, rows, 2 * dh), F32),
        compiler_params=_cparams(("parallel", "arbitrary")),
        name="sb_sample",
    )(page_table, qbd, bias_rows, knew, vnew, *kv_args)
    o = out.reshape(bsz, npair, 2, s, 2, dh)
    o = jnp.stack([o[:, :, 0, :, 0], o[:, :, 1, :, 1]], axis=2)
    return o.transpose(0, 3, 1, 2, 4).reshape(bsz, s, heads * dh)


def _split3(x):
    h1 = x.astype(BF16)
    r1 = x - h1.astype(F32)
    h2 = r1.astype(BF16)
    h3 = (r1 - h2.astype(F32)).astype(BF16)
    return h1, h2, h3


def _ssd_kernel(xs_ref, bm_ref, cm_ref, pxs_ref, pbm_ref, pcm_ref, wxs_ref, wbm_ref, wcm_ref,
                bxs_ref, bbm_ref, bcm_ref, dt_ref, dtb_ref, alog_ref, dskip_ref, selx_ref, sell_ref, z_ref, nw_ref,
                h0_ref,
                y_ref, hl_ref, state_ref, winx_ref, winb_ref, winc_ref, *, chunk, t_valid, hpg, hdim):
    c = pl.program_id(2)
    halo = SUBLANES

    @pl.when(c == 0)
    def _():
        state_ref[...] = h0_ref[...]
        winx_ref[0:halo, :] = pxs_ref[...]
        winb_ref[0:halo, :] = pbm_ref[...]
        winc_ref[0:halo, :] = pcm_ref[...]

    valid = (lax.broadcasted_iota(jnp.int32, (chunk, 1), 0) + c * chunk) < t_valid

    def conv_silu(win_ref, cur_ref, w_ref, b_ref):
        cur = cur_ref[...]
        win_ref[halo:halo + chunk, :] = cur
        out = b_ref[...]
        for i in range(CONV_WIDTH):
            lo = halo - (CONV_WIDTH - 1) + i
            out = out + win_ref[lo:lo + chunk, :] * w_ref[i:i + 1, :]
        win_ref[0:halo, :] = cur[chunk - halo:chunk, :]
        return jnp.where(valid, out * jax.nn.sigmoid(out), 0.0)

    xs = conv_silu(winx_ref, xs_ref, wxs_ref, bxs_ref)
    bm = conv_silu(winb_ref, bm_ref, wbm_ref, bbm_ref)
    cm = conv_silu(winc_ref, cm_ref, wcm_ref, bcm_ref)
    bm16 = bm.astype(BF16)
    cm16 = cm.astype(BF16)

    dt = jnp.where(valid, _softplus(dt_ref[...] + dtb_ref[...]), 0.0)
    da = dt * (-jnp.exp(alog_ref[...]))
    ri = lax.broadcasted_iota(jnp.int32, (chunk, chunk), 0)
    ci = lax.broadcasted_iota(jnp.int32, (chunk, chunk), 1)
    causal = ri >= ci
    tril = jnp.where(causal, 1.0, 0.0).astype(BF16)
    acs = sum(jnp.dot(tril, part, preferred_element_type=F32) for part in _split3(da))
    acs_t = acs.T
    acs_parts = _split3(acs)

    def expand(parts, sel_ref):
        return sum(jnp.dot(part, sel_ref[...], preferred_element_type=F32) for part in parts)

    dt_x = expand(_split3(dt)[:2], selx_ref)
    a_x = expand(acs_parts, selx_ref)
    a_l = expand(acs_parts, sell_ref)
    a_end = a_x[chunk - 1:chunk, :]
    xdt = xs * dt_x
    st_all = state_ref[...].reshape(hpg * hdim, state_ref.shape[2])
    cb = lax.dot_general(cm16, bm16, (((1,), (1,)), ((), ())), preferred_element_type=F32)
    y_off = lax.dot_general(cm16, st_all.astype(BF16), (((1,), (1,)), ((), ())),
                            preferred_element_type=F32) * jnp.exp(a_x)
    upd = lax.dot_general((xdt * jnp.exp(a_end - a_x)).astype(BF16), bm16, (((0,), (0,)), ((), ())),
                          preferred_element_type=F32)
    hpb = LANES // hdim
    lane_head = lax.broadcasted_iota(jnp.int32, (1, LANES), 1) // hdim
    y_blocks = []
    for blk in range(hpg // hpb):
        xb = xdt[:, blk * LANES:(blk + 1) * LANES]
        yb = None
        for j in range(hpb):
            r = blk * hpb + j
            decay = jnp.exp(jnp.where(causal, a_l[:, r * LANES:(r + 1) * LANES] - acs_t[r:r + 1, :], -jnp.inf))
            part = jnp.dot((cb * decay).astype(BF16), jnp.where(lane_head == j, xb, 0.0).astype(BF16),
                           preferred_element_type=F32)
            yb = part if yb is None else yb + part
        y_blocks.append(yb)
    for r in range(hpg):
        a_last = acs[chunk - 1:chunk, r:r + 1]
        state_ref[r] = jnp.exp(a_last) * state_ref[r] + upd[r * hdim:(r + 1) * hdim, :]
    y = jnp.concatenate(y_blocks, axis=1) + y_off + xs * dskip_ref[...]
    zz = z_ref[...].astype(F32)
    gated = y * (zz * jax.nn.sigmoid(zz))
    gated = gated * lax.rsqrt(jnp.mean(gated * gated, axis=-1, keepdims=True) + RMS_EPS)
    y_ref[...] = (gated * nw_ref[...]).astype(y_ref.dtype)

    @pl.when(c == pl.num_programs(2) - 1)
    def _():
        hl_ref[...] = state_ref[...]


def ssd_mix(xbc, prev, dt, z, h0, conv_w, conv_b, dt_bias, a_log, d_skip, norm_w, *, t_valid, groups, chunk=128):
    bsz, t, _ = xbc.shape
    heads, hdim, n = h0.shape[1:]
    width = heads * hdim
    hpg = heads // groups
    gw = width // groups
    chunk = _pick(t, chunk)
    nchunk = -(-t_valid // chunk)
    assert gw % LANES == 0 and n == LANES
    xoff, boff, coff = 0, width // n, width // n + groups

    def rows(w, off):
        return pl.BlockSpec((None, chunk, w), lambda bi, g, c: (bi, c, g + off) if w == n else (bi, c, g))

    def first(w, off):
        return pl.BlockSpec((None, SUBLANES, w), lambda bi, g, c: (bi, 0, g + off) if w == n else (bi, 0, g))

    def par(r, w, off):
        return pl.BlockSpec((r, w), lambda bi, g, c: (0, g + off) if w == n else (0, g))

    lane = pl.BlockSpec((1, LANES), lambda bi, g, c: (0, g))
    head_of = jnp.arange(LANES)[:, None]
    selx = (head_of == jnp.arange(gw)[None, :] // hdim).astype(BF16)
    sell = (head_of == jnp.arange(hpg * LANES)[None, :] // LANES).astype(BF16)
    dskip_x = jnp.repeat(d_skip.astype(F32), hdim).reshape(1, width)
    st_spec = pl.BlockSpec((None, hpg, hdim, n), lambda bi, g, c: (bi, g, 0, 0))
    return pl.pallas_call(
        functools.partial(_ssd_kernel, chunk=chunk, t_valid=t_valid, hpg=hpg, hdim=hdim),
        grid=(bsz, groups, nchunk),
        in_specs=[rows(gw, xoff), rows(n, boff), rows(n, coff),
                  first(gw, xoff), first(n, boff), first(n, coff),
                  par(CONV_WIDTH, gw, xoff), par(CONV_WIDTH, n, boff), par(CONV_WIDTH, n, coff),
                  par(1, gw, xoff), par(1, n, boff), par(1, n, coff),
                  pl.BlockSpec((None, chunk, LANES), lambda bi, g, c: (bi, c, g)),
                  lane, lane,
                  pl.BlockSpec((1, gw), lambda bi, g, c: (0, g)),
                  pl.BlockSpec((LANES, gw), lambda bi, g, c: (0, 0)),
                  pl.BlockSpec((LANES, hpg * LANES), lambda bi, g, c: (0, 0)),
                  pl.BlockSpec((None, chunk, gw), lambda bi, g, c: (bi, c, g)),
                  pl.BlockSpec((1, gw), lambda bi, g, c: (0, g)),
                  st_spec],
        out_specs=[pl.BlockSpec((None, chunk, gw), lambda bi, g, c: (bi, c, g)), st_spec],
        out_shape=[jax.ShapeDtypeStruct((bsz, t, width), BF16), jax.ShapeDtypeStruct((bsz, heads, hdim, n), F32)],
        scratch_shapes=[pltpu.VMEM((hpg, hdim, n), F32),
                        pltpu.VMEM((chunk + SUBLANES, gw), F32),
                        pltpu.VMEM((chunk + SUBLANES, n), F32),
                        pltpu.VMEM((chunk + SUBLANES, n), F32)],
        compiler_params=_cparams(("parallel", "parallel", "arbitrary")),
        name="ssd_mix",
    )(xbc, xbc, xbc, prev, prev, prev, conv_w, conv_w, conv_w, conv_b, conv_b, conv_b,
      dt, dt_bias, a_log, dskip_x, selx, sell, z, norm_w, h0)


def _group_lanes(v, groups):
    hpg = v.shape[0] // groups
    return jnp.pad(v.astype(F32).reshape(groups, hpg), ((0, 0), (0, LANES - hpg))).reshape(1, groups * LANES)


def kernel(x_prompt, x_sample, cache_k, cache_v, state_conv, state_ssm, page_table, meta_tokens,
           ln1_g, ln1_b, ffn1_w1, ffn1_w3, ffn1_w2, w_in, sb_bias, conv_w, conv_b, dt_bias, a_log, d_skip,
           ssm_norm_w, w_gate, b_gate, w_pa, w_pb, w_out, ln2_g, ln2_b,
           ffn2_w1, ffn2_w3, ffn2_w2, ln3_g, ln3_b):
    depth = w_in.shape[0]
    assert depth == 1
    bsz, seq, d = x_prompt.shape
    dec_b, dec_s, _ = x_sample.shape
    n_meta = meta_tokens.shape[0]
    sb_heads, sb_dh = cache_k.shape[3:]
    sb_width = sb_heads * sb_dh
    page = cache_k.shape[2]
    ssm_heads, ssm_hdim, ssm_n = state_ssm.shape[2:]
    ssm_width = ssm_heads * ssm_hdim
    conv_dim = conv_w.shape[2]
    groups = (conv_dim - ssm_width) // (2 * ssm_n)
    hpg = ssm_heads // groups
    alpha = (2.0 * depth) ** 0.25
    scale = sb_dh ** -0.5 * LOG2E
    t_real = seq + n_meta
    row_tile = 256
    t_pad = -(-t_real // row_tile) * row_tile
    chunk = 128

    l = 0
    bf = lambda w: w.astype(BF16)
    vec = lambda v: v.astype(F32).reshape(1, -1)
    w1a, w3a, w2a = bf(ffn1_w1[l]), bf(ffn1_w3[l]), bf(ffn1_w2[l])
    w1b, w3b, w2b = bf(ffn2_w1[l]), bf(ffn2_w3[l]), bf(ffn2_w2[l])
    cuts = [0, sb_width, 2 * sb_width, 3 * sb_width, 3 * sb_width + ssm_width, 3 * sb_width + ssm_width + conv_dim]
    wq, wk, wv, wz, wx = (bf(w_in[l][:, cuts[i]:cuts[i + 1]]) for i in range(5))
    wdt = w_in[l][:, cuts[5]:].reshape(d, groups, hpg)
    wdt = bf(jnp.pad(wdt, ((0, 0), (0, 0), (0, LANES - hpg))).reshape(d, groups * LANES))
    wg, wpa, wpb, wo = bf(w_gate[l]), bf(w_pa[l]), bf(w_pb[l]), bf(w_out[l])
    dtb_l, alog_l = (_group_lanes(v[l], groups) for v in (dt_bias, a_log))
    cw, cb = conv_w[l].astype(F32), vec(conv_b[l])
    bias = sb_bias[l].astype(F32) * LOG2E

    def pre(h):
        return ffn_ln(h, w1a, w3a, w2a, vec(ln1_g[l]), vec(ln1_b[l]), [F32, BF16], alpha=alpha)

    def project(u16, t_out):
        q, = mm(u16, wq, [(BF16, u16.shape[1])], scale=scale)
        k32, k16 = mm(u16, wk, [(F32, t_out), (BF16, u16.shape[1])])
        v32, v16 = mm(u16, wv, [(F32, t_out), (BF16, u16.shape[1])])
        z16, = mm(u16, wz, [(BF16, u16.shape[1])])
        xbc, = mm(u16, wx, [(F32, u16.shape[1])])
        dt, = mm(u16, wdt, [(F32, u16.shape[1])])
        return q, k32, k16, v32, v16, z16, xbc, dt

    def post(u32, u16, o_att, y_ssm):
        h32 = merge_ln(u32, u16, o_att, y_ssm, wg, vec(b_gate[l]), wpa, wpb, wo,
                       vec(ln2_g[l]), vec(ln2_b[l]), alpha=alpha)
        out, = ffn_ln(h32, w1b, w3b, w2b, vec(ln3_g[l]), vec(ln3_b[l]), [F32], alpha=alpha)
        return out

    def ssm(xbc, prev, dt, z16, h0, t_valid):
        return ssd_mix(xbc, prev, dt, z16, h0, cw, cb, dtb_l, alog_l, d_skip[l], vec(ssm_norm_w[l]),
                       t_valid=t_valid, groups=groups, chunk=chunk)

    meta = jnp.broadcast_to(meta_tokens.astype(x_prompt.dtype)[None], (bsz, n_meta, d))
    hp = jnp.concatenate([meta, x_prompt, jnp.zeros((bsz, t_pad - t_real, d), x_prompt.dtype)], axis=1)
    u32, u16 = pre(hp)
    q, k32, k16, v32, v16, z16, xbc, dt = project(u16, t_real)
    o_att = sb_prompt(q, k16, v16, bias, heads=sb_heads)
    y_ssm, h_last_p = ssm(xbc, jnp.zeros((bsz, SUBLANES, conv_dim), F32), dt, z16,
                          jnp.zeros((bsz, ssm_heads, ssm_hdim, ssm_n), F32), t_real)
    y_prompt = post(u32, u16, o_att, y_ssm)[:, n_meta:t_real]
    k_prompt = k32.reshape(1, bsz, t_real, sb_heads, sb_dh)
    v_prompt = v32.reshape(1, bsz, t_real, sb_heads, sb_dh)
    conv_prompt = xbc[:, t_real - (CONV_WIDTH - 1):t_real][None]
    ssm_prompt = h_last_p[None]

    rows = dec_b * dec_s
    u32, u16 = pre(x_sample.reshape(1, rows, d))
    q, k32, _, v32, _, z16, xbc, dt = project(u16, rows)
    k_new = k32.reshape(dec_b, dec_s, sb_heads, sb_dh)
    v_new = v32.reshape(dec_b, dec_s, sb_heads, sb_dh)
    o_att = sb_sample(q.reshape(dec_b, dec_s, sb_heads, sb_dh), k_new, v_new, cache_k[l], cache_v[l], page_table, bias)
    per_seq = lambda a: jnp.pad(a.reshape(dec_b, dec_s, a.shape[-1]), ((0, 0), (0, chunk - dec_s), (0, 0)))
    xbc_s = xbc.reshape(dec_b, dec_s, conv_dim)
    prev = jnp.pad(state_conv[l].astype(F32), ((0, 0), (SUBLANES - (CONV_WIDTH - 1), 0), (0, 0)))
    y_ssm, h_last_s = ssm(per_seq(xbc), prev, per_seq(dt), per_seq(z16), state_ssm[l].astype(F32), dec_s)
    y_ssm = y_ssm[:, :dec_s].reshape(1, rows, ssm_width)
    y_sample = post(u32, u16, o_att.astype(BF16).reshape(1, rows, sb_width), y_ssm).reshape(dec_b, dec_s, d)
    tail = jnp.concatenate([state_conv[l].astype(F32), xbc_s], axis=1)
    conv_sample = tail[:, tail.shape[1] - (CONV_WIDTH - 1):][None]

    return (y_prompt, y_sample, k_prompt, v_prompt, conv_prompt, ssm_prompt,
            k_new[None], v_new[None], conv_sample, h_last_s[None])
```

```python
import functools

import jax
import jax.numpy as jnp
from jax import lax
from jax.experimental import pallas as pl
from jax.experimental.pallas import tpu as pltpu

F32 = jnp.float32
BF16 = jnp.bfloat16
LN_EPS = 1e-5
RMS_EPS = 1e-5
LANES = 128
SUBLANES = 8
CONV_WIDTH = 4
VMEM_LIMIT = 56 * 1024 * 1024


def _cparams(sem):
    return pltpu.CompilerParams(dimension_semantics=sem, vmem_limit_bytes=VMEM_LIMIT)


def _pick(n, pref):
    if n <= pref:
        return n
    for step in (LANES, SUBLANES, 1):
        t = pref - pref % step
        while t >= step:
            if n % t == 0:
                return t
            t -= step
    return n


def _layer_norm(r, g, b):
    mu = jnp.mean(r, axis=-1, keepdims=True)
    xc = r - mu
    var = jnp.mean(xc * xc, axis=-1, keepdims=True)
    return xc * lax.rsqrt(var + LN_EPS) * g + b


def _softplus(z):
    return jnp.maximum(z, 0.0) + jnp.log(1.0 + jnp.exp(-jnp.abs(z)))


def _ffn_ln_kernel(x_ref, w1_ref, w3_ref, w2_ref, g_ref, b_ref, *rest, alpha):
    o_refs, (xb_ref, acc_ref) = rest[:-2], rest[-2:]
    f = pl.program_id(2)

    @pl.when(f == 0)
    def _():
        xb_ref[...] = x_ref[...].astype(BF16)
        acc_ref[...] = jnp.zeros_like(acc_ref)

    xb = xb_ref[...]
    a = jnp.dot(xb, w1_ref[...], preferred_element_type=F32)
    c = jnp.dot(xb, w3_ref[...], preferred_element_type=F32)
    hid = (a * jax.nn.sigmoid(a) * c).astype(BF16)
    acc_ref[...] += jnp.dot(hid, w2_ref[...], preferred_element_type=F32)

    @pl.when(f == pl.num_programs(2) - 1)
    def _():
        r = alpha * x_ref[...].astype(F32) + 0.5 * acc_ref[...]
        y = _layer_norm(r, g_ref[...], b_ref[...])
        for o_ref in o_refs:
            o_ref[...] = y.astype(o_ref.dtype)


def ffn_ln(x, w1, w3, w2, g, b, out_dtypes, *, alpha, tm=768, tf=256):
    bsz, t, d = x.shape
    dff = w1.shape[1]
    tm = _pick(t, tm)
    tf = _pick(dff, tf)
    grid = (bsz, t // tm, dff // tf)
    row = pl.BlockSpec((None, tm, d), lambda bi, i, f: (bi, i, 0))
    vec = pl.BlockSpec((1, d), lambda bi, i, f: (0, 0))
    return pl.pallas_call(
        functools.partial(_ffn_ln_kernel, alpha=alpha),
        grid=grid,
        in_specs=[row,
                  pl.BlockSpec((d, tf), lambda bi, i, f: (0, f)),
                  pl.BlockSpec((d, tf), lambda bi, i, f: (0, f)),
                  pl.BlockSpec((tf, d), lambda bi, i, f: (f, 0)),
                  vec, vec],
        out_specs=[row for _ in out_dtypes],
        out_shape=[jax.ShapeDtypeStruct((bsz, t, d), dt) for dt in out_dtypes],
        scratch_shapes=[pltpu.VMEM((tm, d), BF16), pltpu.VMEM((tm, d), F32)],
        compiler_params=_cparams(("parallel", "parallel", "arbitrary")),
        name="ffn_ln",
    )(x, w1, w3, w2, g, b)


def _mm_kernel(x_ref, w_ref, *o_refs, scale):
    acc = jnp.dot(x_ref[...], w_ref[...], preferred_element_type=F32)
    if scale != 1.0:
        acc = acc * scale
    for o_ref in o_refs:
        o_ref[...] = acc.astype(o_ref.dtype)


def mm(x, w, outs, *, scale=1.0, tm=1408, tn=512):
    bsz, t, d = x.shape
    n = w.shape[1]
    tm = _pick(t, tm)
    tn = _pick(n, tn)
    grid = (bsz, t // tm, n // tn)
    oblk = pl.BlockSpec((None, tm, tn), lambda bi, i, j: (bi, i, j))
    return pl.pallas_call(
        functools.partial(_mm_kernel, scale=scale),
        grid=grid,
        in_specs=[pl.BlockSpec((None, tm, d), lambda bi, i, j: (bi, i, 0)),
                  pl.BlockSpec((d, tn), lambda bi, i, j: (0, j))],
        out_specs=[oblk for _ in outs],
        out_shape=[jax.ShapeDtypeStruct((bsz, rows, n), dt) for dt, rows in outs],
        compiler_params=_cparams(("parallel", "parallel", "arbitrary")),
        name="proj",
    )(x, w)


def _merge_kernel(u32_ref, u16_ref, oa_ref, ys_ref, wga_ref, wgb_ref, bga_ref, bgb_ref, wpa_ref, wpb_ref,
                  wout_ref, g_ref, b_ref, o32_ref, acc_ref, *, alpha):
    c = pl.program_id(2)

    @pl.when(c == 0)
    def _():
        acc_ref[...] = jnp.zeros_like(acc_ref)

    u = u16_ref[...]
    ga = jax.nn.sigmoid(jnp.dot(u, wga_ref[...], preferred_element_type=F32) + bga_ref[...])
    gb = jax.nn.sigmoid(jnp.dot(u, wgb_ref[...], preferred_element_type=F32) + bgb_ref[...])
    ya = jnp.dot(oa_ref[...], wpa_ref[...], preferred_element_type=F32)
    yb = jnp.dot(ys_ref[...], wpb_ref[...], preferred_element_type=F32)
    mix = (ga * ya + gb * yb).astype(BF16)
    acc_ref[...] += jnp.dot(mix, wout_ref[...], preferred_element_type=F32)

    @pl.when(c == pl.num_programs(2) - 1)
    def _():
        r = alpha * u32_ref[...] + acc_ref[...]
        o32_ref[...] = _layer_norm(r, g_ref[...], b_ref[...])


def merge_ln(u32, u16, oa, ys, w_gate, b_gate, w_pa, w_pb, w_out, g, b, *, alpha, tm=384, tc=512):
    bsz, t, d = u32.shape
    tm = _pick(t, tm)
    tc = _pick(d, tc)
    nc = d // tc
    grid = (bsz, t // tm, nc)
    row = pl.BlockSpec((None, tm, d), lambda bi, i, c: (bi, i, 0))
    vec = pl.BlockSpec((1, d), lambda bi, i, c: (0, 0))
    col = pl.BlockSpec((d, tc), lambda bi, i, c: (0, c))
    col_hi = pl.BlockSpec((d, tc), lambda bi, i, c: (0, c + nc))
    bias = pl.BlockSpec((1, tc), lambda bi, i, c: (0, c))
    bias_hi = pl.BlockSpec((1, tc), lambda bi, i, c: (0, c + nc))
    return pl.pallas_call(
        functools.partial(_merge_kernel, alpha=alpha),
        grid=grid,
        in_specs=[row, row, row, row, col, col_hi, bias, bias_hi, col, col,
                  pl.BlockSpec((tc, d), lambda bi, i, c: (c, 0)), vec, vec],
        out_specs=row,
        out_shape=jax.ShapeDtypeStruct((bsz, t, d), F32),
        scratch_shapes=[pltpu.VMEM((tm, d), F32)],
        compiler_params=_cparams(("parallel", "parallel", "arbitrary")),
        name="merge_ln",
    )(u32, u16, oa, ys, w_gate, w_gate, b_gate, b_gate, w_pa, w_pb, w_out, g, b)


def _neg_suffix_matrix(tk):
    r = lax.broadcasted_iota(jnp.int32, (tk, tk), 0)
    c = lax.broadcasted_iota(jnp.int32, (tk, tk), 1)
    return jnp.where(r > c, -1.0, 0.0).astype(BF16)


LOG2E = 1.4426950408889634


def _sb_tile(z2, carry, uneg, mask):
    sp = jnp.maximum(z2, 0.0) + jnp.log(1.0 + jnp.exp2(-jnp.abs(z2))) * LOG2E
    if mask is not None:
        sp = jnp.where(mask, sp, 0.0)
    after = jnp.dot(sp.astype(BF16), uneg, preferred_element_type=F32)
    a = jnp.exp2(z2 - sp + after + carry)
    if mask is not None:
        a = jnp.where(mask, a, 0.0)
    return a, carry - jnp.sum(sp, axis=-1, keepdims=True)


def _sb_prompt_kernel(bias_ref, q_ref, k_ref, v_ref, o_ref, acc_ref, z_ref, w_ref, *, tq, tk):
    qi = pl.program_id(2)
    q = q_ref[...]
    bias = bias_ref[pl.program_id(1)]
    uneg = _neg_suffix_matrix(tk)
    ratio = tq // tk

    def put_logits(kblk):
        start = pl.multiple_of(kblk * tk, tk)
        z_ref[kblk % 2] = lax.dot_general(q, k_ref[pl.ds(start, tk), :], (((1,), (1,)), ((), ())),
                                          preferred_element_type=F32) + bias

    def weigh(kblk):
        start = pl.multiple_of(kblk * tk, tk)
        acc_ref[...] += jnp.dot(w_ref[kblk % 2], v_ref[pl.ds(start, tk), :], preferred_element_type=F32)

    def sub_step(kblk, carry, mask, weigh_previous):
        slot = kblk % 2
        z = z_ref[slot]
        put_logits(jnp.maximum(kblk - 1, 0))
        if weigh_previous:
            weigh(kblk + 1)
        a, carry = _sb_tile(z, carry, uneg, mask)
        w_ref[slot] = a.astype(BF16)
        return carry

    acc_ref[...] = jnp.zeros_like(acc_ref)
    qpos = lax.broadcasted_iota(jnp.int32, (tq, tk), 0)
    kloc = lax.broadcasted_iota(jnp.int32, (tq, tk), 1)
    first = qi * ratio
    carry = jnp.zeros((tq, 1), F32)
    put_logits(first + ratio - 1)
    for dblk in reversed(range(ratio)):
        carry = sub_step(first + dblk, carry, (kloc + dblk * tk) < qpos, dblk != ratio - 1)
    lax.fori_loop(0, first, lambda i, c: sub_step(first - 1 - i, c, None, True), carry)
    weigh(0)
    o_ref[...] = acc_ref[...].astype(o_ref.dtype)


def sb_prompt(q, k, v, bias2, *, heads, tq=256, tk=256):
    bsz, t, width = q.shape
    dh = width // heads
    tq = _pick(t, tq)
    tk = _pick(tq, tk)
    return pl.pallas_call(
        functools.partial(_sb_prompt_kernel, tq=tq, tk=tk),
        grid=(bsz, heads, t // tq),
        in_specs=[pl.BlockSpec(memory_space=pltpu.SMEM),
                  pl.BlockSpec((None, tq, dh), lambda bi, h, i: (bi, i, h)),
                  pl.BlockSpec((None, t, dh), lambda bi, h, i: (bi, 0, h)),
                  pl.BlockSpec((None, t, dh), lambda bi, h, i: (bi, 0, h))],
        out_specs=pl.BlockSpec((None, tq, dh), lambda bi, h, i: (bi, i, h)),
        out_shape=jax.ShapeDtypeStruct((bsz, t, width), BF16),
        scratch_shapes=[pltpu.VMEM((tq, dh), F32), pltpu.VMEM((2, tq, tk), F32), pltpu.VMEM((2, tq, tk), BF16)],
        compiler_params=_cparams(("parallel", "parallel", "arbitrary")),
        name="sb_prompt",
    )(bias2, q, k, v)


HEAD_GROUPS = 4
PAGES_PER_CHAIN = 4


def _sb_sample_kernel(pt_ref, qg_ref, bias_ref, knew_ref, vnew_ref, *rest, pages_per_step, heads, dec_seq, page):
    kv_refs = rest[:2 * pages_per_step]
    o_ref, acc_ref, carry_ref = rest[2 * pages_per_step:]
    s = pl.program_id(1)
    hg = heads // HEAD_GROUPS
    ppt = LANES // hg
    rows = qg_ref.shape[1]
    uneg = _neg_suffix_matrix(LANES)
    bias = bias_ref[...]
    r_i = lax.broadcasted_iota(jnp.int32, (HEAD_GROUPS * rows, LANES), 0) % rows
    c_i = lax.broadcasted_iota(jnp.int32, (HEAD_GROUPS * rows, LANES), 1)
    valid = (r_i < hg * dec_seq) & ((c_i % hg) == (r_i // dec_seq))

    def tile_rows(ref, c, tl):
        return ref[pl.ds(c + tl * ppt * heads, LANES, stride=HEAD_GROUPS), :].astype(BF16)

    def do_tiles(tiles, mask):
        carry = carry_ref[...]
        zs = [jnp.concatenate([lax.dot_general(qg_ref[c], tile_rows(k_ref, c, tl), (((1,), (1,)), ((), ())),
                                               preferred_element_type=F32) for c in range(HEAD_GROUPS)], axis=0) + bias
              for k_ref, _, tl in tiles]
        sps = [jnp.where(mask, jnp.maximum(z, 0.0) + jnp.log(1.0 + jnp.exp2(-jnp.abs(z))) * LOG2E, 0.0) for z in zs]
        afters = [jnp.dot(sp.astype(BF16), uneg, preferred_element_type=F32) for sp in sps]
        parts = [None] * HEAD_GROUPS
        for (_, v_ref, tl), z, sp, after in zip(tiles, zs, sps, afters):
            a = jnp.where(mask, jnp.exp2(z - sp + after + carry), 0.0)
            carry = carry - jnp.sum(sp, axis=-1, keepdims=True)
            for c in range(HEAD_GROUPS):
                part = jnp.dot(a[c * rows:(c + 1) * rows].astype(BF16), tile_rows(v_ref, c, tl),
                               preferred_element_type=F32)
                parts[c] = part if parts[c] is None else parts[c] + part
        carry_ref[...] = carry
        for c in range(HEAD_GROUPS):
            acc_ref[c] += parts[c]

    @pl.when(s == 0)
    def _():
        acc_ref[...] = jnp.zeros_like(acc_ref)
        carry_ref[...] = jnp.zeros_like(carry_ref)
        do_tiles([(knew_ref, vnew_ref, 0)], valid & ((c_i // hg) < (r_i % dec_seq)))

    for p in range(0, pages_per_step, PAGES_PER_CHAIN):
        do_tiles([(kv_refs[2 * pp], kv_refs[2 * pp + 1], tl)
                  for pp in range(p, min(p + PAGES_PER_CHAIN, pages_per_step))
                  for tl in reversed(range(page // ppt))], valid)

    @pl.when(s == pl.num_programs(1) - 1)
    def _():
        o_ref[...] = acc_ref[...]


def sb_sample(q, k_new, v_new, cache_k, cache_v, page_table, bias2, *, pages_per_step=8):
    bsz, s, heads, dh = q.shape
    n_phys, page = cache_k.shape[:2]
    n_pages = page_table.shape[1]
    pps = _pick(n_pages, pages_per_step)
    hg = heads // HEAD_GROUPS
    ppt = LANES // hg
    assert heads % HEAD_GROUPS == 0 and LANES % hg == 0 and page % ppt == 0 and s <= ppt
    rows = -(-hg * s // 16) * 16
    qg = q.transpose(0, 2, 1, 3).reshape(bsz, hg, HEAD_GROUPS, s, dh).transpose(0, 2, 1, 3, 4)
    qg = jnp.pad(qg.reshape(bsz, HEAD_GROUPS, hg * s, dh), ((0, 0), (0, 0), (0, rows - hg * s), (0, 0)))
    bg = jnp.broadcast_to(bias2.astype(F32).reshape(hg, HEAD_GROUPS).T[:, :, None], (HEAD_GROUPS, hg, s))
    bias_rows = jnp.pad(bg.reshape(HEAD_GROUPS, hg * s), ((0, 0), (0, rows - hg * s)))
    bias_rows = jnp.broadcast_to(bias_rows.reshape(HEAD_GROUPS * rows, 1), (HEAD_GROUPS * rows, LANES))
    pad = ((0, 0), (0, ppt - s), (0, 0), (0, 0))
    knew = jnp.pad(k_new, pad).reshape(bsz, ppt * heads, dh)
    vnew = jnp.pad(v_new, pad).reshape(bsz, ppt * heads, dh)
    ck = cache_k.reshape(n_phys, page * heads, dh)
    cv = cache_v.reshape(n_phys, page * heads, dh)

    def page_spec(p):
        return pl.BlockSpec((None, page * heads, dh),
                            lambda bi, st, pt: (pt[bi, n_pages - 1 - (st * pps + p)], 0, 0))

    kv_specs, kv_args = [], []
    for p in range(pps):
        kv_specs += [page_spec(p), page_spec(p)]
        kv_args += [ck, cv]
    new_spec = pl.BlockSpec((None, ppt * heads, dh), lambda bi, st, pt: (bi, 0, 0))
    grp_spec = pl.BlockSpec((None, HEAD_GROUPS, rows, dh), lambda bi, st, pt: (bi, 0, 0, 0))
    out = pl.pallas_call(
        functools.partial(_sb_sample_kernel, pages_per_step=pps, heads=heads, dec_seq=s, page=page),
        grid_spec=pltpu.PrefetchScalarGridSpec(
            num_scalar_prefetch=1,
            grid=(bsz, n_pages // pps),
            in_specs=[grp_spec,
                      pl.BlockSpec((HEAD_GROUPS * rows, LANES), lambda bi, st, pt: (0, 0)),
                      new_spec, new_spec] + kv_specs,
            out_specs=grp_spec,
            scratch_shapes=[pltpu.VMEM((HEAD_GROUPS, rows, dh), F32), pltpu.VMEM((HEAD_GROUPS * rows, 1), F32)],
        ),
        out_shape=jax.ShapeDtypeStruct((bsz, HEAD_GROUPS, rows, dh), F32),
        compiler_params=_cparams(("parallel", "arbitrary")),
        name="sb_sample",
    )(page_table, qg, bias_rows, knew, vnew, *kv_args)
    o = out[:, :, :hg * s].reshape(bsz, HEAD_GROUPS, hg, s, dh)
    return o.transpose(0, 3, 2, 1, 4).reshape(bsz, s, heads * dh)


def _split3(x):
    h1 = x.astype(BF16)
    r1 = x - h1.astype(F32)
    h2 = r1.astype(BF16)
    h3 = (r1 - h2.astype(F32)).astype(BF16)
    return h1, h2, h3


def _ssd_kernel(xs_ref, bm_ref, cm_ref, pxs_ref, pbm_ref, pcm_ref, wxs_ref, wbm_ref, wcm_ref,
                bxs_ref, bbm_ref, bcm_ref, dt_ref, dtb_ref, alog_ref, dskip_ref, selx_ref, sell_ref, z_ref, nw_ref,
                h0_ref,
                y_ref, hl_ref, state_ref, winx_ref, winb_ref, winc_ref, *, chunk, t_valid, hpg, hdim):
    c = pl.program_id(2)
    halo = SUBLANES

    @pl.when(c == 0)
    def _():
        state_ref[...] = h0_ref[...]
        winx_ref[0:halo, :] = pxs_ref[...]
        winb_ref[0:halo, :] = pbm_ref[...]
        winc_ref[0:halo, :] = pcm_ref[...]

    valid = (lax.broadcasted_iota(jnp.int32, (chunk, 1), 0) + c * chunk) < t_valid

    def conv_silu(win_ref, cur_ref, w_ref, b_ref):
        cur = cur_ref[...]
        win_ref[halo:halo + chunk, :] = cur
        out = b_ref[...]
        for i in range(CONV_WIDTH):
            lo = halo - (CONV_WIDTH - 1) + i
            out = out + win_ref[lo:lo + chunk, :] * w_ref[i:i + 1, :]
        win_ref[0:halo, :] = cur[chunk - halo:chunk, :]
        return jnp.where(valid, out * jax.nn.sigmoid(out), 0.0)

    xs = conv_silu(winx_ref, xs_ref, wxs_ref, bxs_ref)
    bm = conv_silu(winb_ref, bm_ref, wbm_ref, bbm_ref)
    cm = conv_silu(winc_ref, cm_ref, wcm_ref, bcm_ref)
    bm16 = bm.astype(BF16)
    cm16 = cm.astype(BF16)

    dt = jnp.where(valid, _softplus(dt_ref[...] + dtb_ref[...]), 0.0)
    da = dt * (-jnp.exp(alog_ref[...]))
    ri = lax.broadcasted_iota(jnp.int32, (chunk, chunk), 0)
    ci = lax.broadcasted_iota(jnp.int32, (chunk, chunk), 1)
    causal = ri >= ci
    tril = jnp.where(causal, 1.0, 0.0).astype(BF16)
    acs = sum(jnp.dot(tril, part, preferred_element_type=F32) for part in _split3(da))
    acs_t = acs.T
    acs_parts = _split3(acs)

    def expand(parts, sel_ref):
        return sum(jnp.dot(part, sel_ref[...], preferred_element_type=F32) for part in parts)

    dt_x = expand(_split3(dt)[:2], selx_ref)
    a_x = expand(acs_parts, selx_ref)
    a_l = expand(acs_parts, sell_ref)
    a_end = a_x[chunk - 1:chunk, :]
    xdt = xs * dt_x
    st_all = state_ref[...].reshape(hpg * hdim, state_ref.shape[2])
    cb = lax.dot_general(cm16, bm16, (((1,), (1,)), ((), ())), preferred_element_type=F32)
    y_off = lax.dot_general(cm16, st_all.astype(BF16), (((1,), (1,)), ((), ())),
                            preferred_element_type=F32) * jnp.exp(a_x)
    upd = lax.dot_general((xdt * jnp.exp(a_end - a_x)).astype(BF16), bm16, (((0,), (0,)), ((), ())),
                          preferred_element_type=F32)
    hpb = LANES // hdim
    lane_head = lax.broadcasted_iota(jnp.int32, (1, LANES), 1) // hdim
    y_blocks = []
    for blk in range(hpg // hpb):
        xb = xdt[:, blk * LANES:(blk + 1) * LANES]
        yb = None
        for j in range(hpb):
            r = blk * hpb + j
            decay = jnp.exp(jnp.where(causal, a_l[:, r * LANES:(r + 1) * LANES] - acs_t[r:r + 1, :], -jnp.inf))
            part = jnp.dot((cb * decay).astype(BF16), jnp.where(lane_head == j, xb, 0.0).astype(BF16),
                           preferred_element_type=F32)
            yb = part if yb is None else yb + part
        y_blocks.append(yb)
    for r in range(hpg):
        a_last = acs[chunk - 1:chunk, r:r + 1]
        state_ref[r] = jnp.exp(a_last) * state_ref[r] + upd[r * hdim:(r + 1) * hdim, :]
    y = jnp.concatenate(y_blocks, axis=1) + y_off + xs * dskip_ref[...]
    zz = z_ref[...].astype(F32)
    gated = y * (zz * jax.nn.sigmoid(zz))
    gated = gated * lax.rsqrt(jnp.mean(gated * gated, axis=-1, keepdims=True) + RMS_EPS)
    y_ref[...] = (gated * nw_ref[...]).astype(y_ref.dtype)

    @pl.when(c == pl.num_programs(2) - 1)
    def _():
        hl_ref[...] = state_ref[...]


def ssd_mix(xbc, prev, dt, z, h0, conv_w, conv_b, dt_bias, a_log, d_skip, norm_w, *, t_valid, groups, chunk=128):
    bsz, t, _ = xbc.shape
    heads, hdim, n = h0.shape[1:]
    width = heads * hdim
    hpg = heads // groups
    gw = width // groups
    chunk = _pick(t, chunk)
    nchunk = t // chunk
    assert gw % LANES == 0 and n == LANES
    xoff, boff, coff = 0, width // n, width // n + groups

    def rows(w, off):
        return pl.BlockSpec((None, chunk, w), lambda bi, g, c: (bi, c, g + off) if w == n else (bi, c, g))

    def first(w, off):
        return pl.BlockSpec((None, SUBLANES, w), lambda bi, g, c: (bi, 0, g + off) if w == n else (bi, 0, g))

    def par(r, w, off):
        return pl.BlockSpec((r, w), lambda bi, g, c: (0, g + off) if w == n else (0, g))

    lane = pl.BlockSpec((1, LANES), lambda bi, g, c: (0, g))
    head_of = jnp.arange(LANES)[:, None]
    selx = (head_of == jnp.arange(gw)[None, :] // hdim).astype(BF16)
    sell = (head_of == jnp.arange(hpg * LANES)[None, :] // LANES).astype(BF16)
    dskip_x = jnp.repeat(d_skip.astype(F32), hdim).reshape(1, width)
    st_spec = pl.BlockSpec((None, hpg, hdim, n), lambda bi, g, c: (bi, g, 0, 0))
    return pl.pallas_call(
        functools.partial(_ssd_kernel, chunk=chunk, t_valid=t_valid, hpg=hpg, hdim=hdim),
        grid=(bsz, groups, nchunk),
        in_specs=[rows(gw, xoff), rows(n, boff), rows(n, coff),
                  first(gw, xoff), first(n, boff), first(n, coff),
                  par(CONV_WIDTH, gw, xoff), par(CONV_WIDTH, n, boff), par(CONV_WIDTH, n, coff),
                  par(1, gw, xoff), par(1, n, boff), par(1, n, coff),
                  pl.BlockSpec((None, chunk, LANES), lambda bi, g, c: (bi, c, g)),
                  lane, lane,
                  pl.BlockSpec((1, gw), lambda bi, g, c: (0, g)),
                  pl.BlockSpec((LANES, gw), lambda bi, g, c: (0, 0)),
                  pl.BlockSpec((LANES, hpg * LANES), lambda bi, g, c: (0, 0)),
                  pl.BlockSpec((None, chunk, gw), lambda bi, g, c: (bi, c, g)),
                  pl.BlockSpec((1, gw), lambda bi, g, c: (0, g)),
                  st_spec],
        out_specs=[pl.BlockSpec((None, chunk, gw), lambda bi, g, c: (bi, c, g)), st_spec],
        out_shape=[jax.ShapeDtypeStruct((bsz, t, width), BF16), jax.ShapeDtypeStruct((bsz, heads, hdim, n), F32)],
        scratch_shapes=[pltpu.VMEM((hpg, hdim, n), F32),
                        pltpu.VMEM((chunk + SUBLANES, gw), F32),
                        pltpu.VMEM((chunk + SUBLANES, n), F32),
                        pltpu.VMEM((chunk + SUBLANES, n), F32)],
        compiler_params=_cparams(("parallel", "parallel", "arbitrary")),
        name="ssd_mix",
    )(xbc, xbc, xbc, prev, prev, prev, conv_w, conv_w, conv_w, conv_b, conv_b, conv_b,
      dt, dt_bias, a_log, dskip_x, selx, sell, z, norm_w, h0)


def _group_lanes(v, groups):
    hpg = v.shape[0] // groups
    return jnp.pad(v.astype(F32).reshape(groups, hpg), ((0, 0), (0, LANES - hpg))).reshape(1, groups * LANES)


def kernel(x_prompt, x_sample, cache_k, cache_v, state_conv, state_ssm, page_table, meta_tokens,
           ln1_g, ln1_b, ffn1_w1, ffn1_w3, ffn1_w2, w_in, sb_bias, conv_w, conv_b, dt_bias, a_log, d_skip,
           ssm_norm_w, w_gate, b_gate, w_pa, w_pb, w_out, ln2_g, ln2_b,
           ffn2_w1, ffn2_w3, ffn2_w2, ln3_g, ln3_b):
    depth = w_in.shape[0]
    assert depth == 1
    bsz, seq, d = x_prompt.shape
    dec_b, dec_s, _ = x_sample.shape
    n_meta = meta_tokens.shape[0]
    sb_heads, sb_dh = cache_k.shape[3:]
    sb_width = sb_heads * sb_dh
    page = cache_k.shape[2]
    ssm_heads, ssm_hdim, ssm_n = state_ssm.shape[2:]
    ssm_width = ssm_heads * ssm_hdim
    conv_dim = conv_w.shape[2]
    groups = (conv_dim - ssm_width) // (2 * ssm_n)
    hpg = ssm_heads // groups
    alpha = (2.0 * depth) ** 0.25
    scale = sb_dh ** -0.5 * LOG2E
    t_real = seq + n_meta
    row_tile = 256
    t_pad = -(-t_real // row_tile) * row_tile
    chunk = 128

    l = 0
    bf = lambda w: w.astype(BF16)
    vec = lambda v: v.astype(F32).reshape(1, -1)
    w1a, w3a, w2a = bf(ffn1_w1[l]), bf(ffn1_w3[l]), bf(ffn1_w2[l])
    w1b, w3b, w2b = bf(ffn2_w1[l]), bf(ffn2_w3[l]), bf(ffn2_w2[l])
    cuts = [0, sb_width, 2 * sb_width, 3 * sb_width, 3 * sb_width + ssm_width, 3 * sb_width + ssm_width + conv_dim]
    wq, wk, wv, wz, wx = (bf(w_in[l][:, cuts[i]:cuts[i + 1]]) for i in range(5))
    wdt = w_in[l][:, cuts[5]:].reshape(d, groups, hpg)
    wdt = bf(jnp.pad(wdt, ((0, 0), (0, 0), (0, LANES - hpg))).reshape(d, groups * LANES))
    wg, wpa, wpb, wo = bf(w_gate[l]), bf(w_pa[l]), bf(w_pb[l]), bf(w_out[l])
    dtb_l, alog_l = (_group_lanes(v[l], groups) for v in (dt_bias, a_log))
    cw, cb = conv_w[l].astype(F32), vec(conv_b[l])
    bias = sb_bias[l].astype(F32) * LOG2E

    def pre(h):
        return ffn_ln(h, w1a, w3a, w2a, vec(ln1_g[l]), vec(ln1_b[l]), [F32, BF16], alpha=alpha)

    def project(u16, t_out):
        q, = mm(u16, wq, [(BF16, u16.shape[1])], scale=scale)
        k32, k16 = mm(u16, wk, [(F32, t_out), (BF16, u16.shape[1])])
        v32, v16 = mm(u16, wv, [(F32, t_out), (BF16, u16.shape[1])])
        z16, = mm(u16, wz, [(BF16, u16.shape[1])])
        xbc, = mm(u16, wx, [(F32, u16.shape[1])])
        dt, = mm(u16, wdt, [(F32, u16.shape[1])])
        return q, k32, k16, v32, v16, z16, xbc, dt

    def post(u32, u16, o_att, y_ssm):
        h32 = merge_ln(u32, u16, o_att, y_ssm, wg, vec(b_gate[l]), wpa, wpb, wo,
                       vec(ln2_g[l]), vec(ln2_b[l]), alpha=alpha)
        out, = ffn_ln(h32, w1b, w3b, w2b, vec(ln3_g[l]), vec(ln3_b[l]), [F32], alpha=alpha)
        return out

    def ssm(xbc, prev, dt, z16, h0, t_valid):
        return ssd_mix(xbc, prev, dt, z16, h0, cw, cb, dtb_l, alog_l, d_skip[l], vec(ssm_norm_w[l]),
                       t_valid=t_valid, groups=groups, chunk=chunk)

    meta = jnp.broadcast_to(meta_tokens.astype(x_prompt.dtype)[None], (bsz, n_meta, d))
    hp = jnp.concatenate([meta, x_prompt, jnp.zeros((bsz, t_pad - t_real, d), x_prompt.dtype)], axis=1)
    u32, u16 = pre(hp)
    q, k32, k16, v32, v16, z16, xbc, dt = project(u16, t_real)
    o_att = sb_prompt(q, k16, v16, bias, heads=sb_heads)
    y_ssm, h_last_p = ssm(xbc, jnp.zeros((bsz, SUBLANES, conv_dim), F32), dt, z16,
                          jnp.zeros((bsz, ssm_heads, ssm_hdim, ssm_n), F32), t_real)
    y_prompt = post(u32, u16, o_att, y_ssm)[:, n_meta:t_real]
    k_prompt = k32.reshape(1, bsz, t_real, sb_heads, sb_dh)
    v_prompt = v32.reshape(1, bsz, t_real, sb_heads, sb_dh)
    conv_prompt = xbc[:, t_real - (CONV_WIDTH - 1):t_real][None]
    ssm_prompt = h_last_p[None]

    rows = dec_b * dec_s
    u32, u16 = pre(x_sample.reshape(1, rows, d))
    q, k32, _, v32, _, z16, xbc, dt = project(u16, rows)
    k_new = k32.reshape(dec_b, dec_s, sb_heads, sb_dh)
    v_new = v32.reshape(dec_b, dec_s, sb_heads, sb_dh)
    o_att = sb_sample(q.reshape(dec_b, dec_s, sb_heads, sb_dh), k_new, v_new, cache_k[l], cache_v[l], page_table, bias)
    per_seq = lambda a: jnp.pad(a.reshape(dec_b, dec_s, a.shape[-1]), ((0, 0), (0, chunk - dec_s), (0, 0)))
    xbc_s = xbc.reshape(dec_b, dec_s, conv_dim)
    prev = jnp.pad(state_conv[l].astype(F32), ((0, 0), (SUBLANES - (CONV_WIDTH - 1), 0), (0, 0)))
    y_ssm, h_last_s = ssm(per_seq(xbc), prev, per_seq(dt), per_seq(z16), state_ssm[l].astype(F32), dec_s)
    y_ssm = y_ssm[:, :dec_s].reshape(1, rows, ssm_width)
    y_sample = post(u32, u16, o_att.astype(BF16).reshape(1, rows, sb_width), y_ssm).reshape(dec_b, dec_s, d)
    tail = jnp.concatenate([state_conv[l].astype(F32), xbc_s], axis=1)
    conv_sample = tail[:, tail.shape[1] - (CONV_WIDTH - 1):][None]

    return (y_prompt, y_sample, k_prompt, v_prompt, conv_prompt, ssm_prompt,
            k_new[None], v_new[None], conv_sample, h_last_s[None])
```

```python
import functools

import jax
import jax.numpy as jnp
from jax import lax
from jax.experimental import pallas as pl
from jax.experimental.pallas import tpu as pltpu

F32 = jnp.float32
BF16 = jnp.bfloat16
LN_EPS = 1e-5
RMS_EPS = 1e-5
LANES = 128
SUBLANES = 8
CONV_WIDTH = 4
VMEM_LIMIT = 56 * 1024 * 1024


def _cparams(sem):
    return pltpu.CompilerParams(dimension_semantics=sem, vmem_limit_bytes=VMEM_LIMIT)


def _pick(n, pref):
    if n <= pref:
        return n
    for step in (LANES, SUBLANES, 1):
        t = pref - pref % step
        while t >= step:
            if n % t == 0:
                return t
            t -= step
    return n


def _layer_norm(r, g, b):
    mu = jnp.mean(r, axis=-1, keepdims=True)
    xc = r - mu
    var = jnp.mean(xc * xc, axis=-1, keepdims=True)
    return xc * lax.rsqrt(var + LN_EPS) * g + b


def _softplus(z):
    return jnp.maximum(z, 0.0) + jnp.log(1.0 + jnp.exp(-jnp.abs(z)))


def _ffn_ln_kernel(x_ref, w1_ref, w3_ref, w2_ref, g_ref, b_ref, *rest, alpha):
    o_refs, (xb_ref, acc_ref) = rest[:-2], rest[-2:]
    f = pl.program_id(2)

    @pl.when(f == 0)
    def _():
        xb_ref[...] = x_ref[...].astype(BF16)
        acc_ref[...] = jnp.zeros_like(acc_ref)

    xb = xb_ref[...]
    a = jnp.dot(xb, w1_ref[...], preferred_element_type=F32)
    c = jnp.dot(xb, w3_ref[...], preferred_element_type=F32)
    hid = (a * jax.nn.sigmoid(a) * c).astype(BF16)
    acc_ref[...] += jnp.dot(hid, w2_ref[...], preferred_element_type=F32)

    @pl.when(f == pl.num_programs(2) - 1)
    def _():
        r = alpha * x_ref[...].astype(F32) + 0.5 * acc_ref[...]
        y = _layer_norm(r, g_ref[...], b_ref[...])
        for o_ref in o_refs:
            o_ref[...] = y.astype(o_ref.dtype)


def ffn_ln(x, w1, w3, w2, g, b, out_dtypes, *, alpha, tm=768, tf=256):
    bsz, t, d = x.shape
    dff = w1.shape[1]
    tm = _pick(t, tm)
    tf = _pick(dff, tf)
    grid = (bsz, t // tm, dff // tf)
    row = pl.BlockSpec((None, tm, d), lambda bi, i, f: (bi, i, 0))
    vec = pl.BlockSpec((1, d), lambda bi, i, f: (0, 0))
    return pl.pallas_call(
        functools.partial(_ffn_ln_kernel, alpha=alpha),
        grid=grid,
        in_specs=[row,
                  pl.BlockSpec((d, tf), lambda bi, i, f: (0, f)),
                  pl.BlockSpec((d, tf), lambda bi, i, f: (0, f)),
                  pl.BlockSpec((tf, d), lambda bi, i, f: (f, 0)),
                  vec, vec],
        out_specs=[row for _ in out_dtypes],
        out_shape=[jax.ShapeDtypeStruct((bsz, t, d), dt) for dt in out_dtypes],
        scratch_shapes=[pltpu.VMEM((tm, d), BF16), pltpu.VMEM((tm, d), F32)],
        compiler_params=_cparams(("parallel", "parallel", "arbitrary")),
        name="ffn_ln",
    )(x, w1, w3, w2, g, b)


def _mm_kernel(x_ref, w_ref, *o_refs, scale):
    acc = jnp.dot(x_ref[...], w_ref[...], preferred_element_type=F32)
    if scale != 1.0:
        acc = acc * scale
    for o_ref in o_refs:
        o_ref[...] = acc.astype(o_ref.dtype)


def mm(x, w, outs, *, scale=1.0, tm=1408, tn=512):
    bsz, t, d = x.shape
    n = w.shape[1]
    tm = _pick(t, tm)
    tn = _pick(n, tn)
    grid = (bsz, t // tm, n // tn)
    oblk = pl.BlockSpec((None, tm, tn), lambda bi, i, j: (bi, i, j))
    return pl.pallas_call(
        functools.partial(_mm_kernel, scale=scale),
        grid=grid,
        in_specs=[pl.BlockSpec((None, tm, d), lambda bi, i, j: (bi, i, 0)),
                  pl.BlockSpec((d, tn), lambda bi, i, j: (0, j))],
        out_specs=[oblk for _ in outs],
        out_shape=[jax.ShapeDtypeStruct((bsz, rows, n), dt) for dt, rows in outs],
        compiler_params=_cparams(("parallel", "parallel", "arbitrary")),
        name="proj",
    )(x, w)


def _merge_kernel(u32_ref, u16_ref, oa_ref, ys_ref, wga_ref, wgb_ref, bga_ref, bgb_ref, wpa_ref, wpb_ref,
                  wout_ref, g_ref, b_ref, o32_ref, acc_ref, *, alpha):
    c = pl.program_id(2)

    @pl.when(c == 0)
    def _():
        acc_ref[...] = jnp.zeros_like(acc_ref)

    u = u16_ref[...]
    ga = jax.nn.sigmoid(jnp.dot(u, wga_ref[...], preferred_element_type=F32) + bga_ref[...])
    gb = jax.nn.sigmoid(jnp.dot(u, wgb_ref[...], preferred_element_type=F32) + bgb_ref[...])
    ya = jnp.dot(oa_ref[...], wpa_ref[...], preferred_element_type=F32)
    yb = jnp.dot(ys_ref[...], wpb_ref[...], preferred_element_type=F32)
    mix = (ga * ya + gb * yb).astype(BF16)
    acc_ref[...] += jnp.dot(mix, wout_ref[...], preferred_element_type=F32)

    @pl.when(c == pl.num_programs(2) - 1)
    def _():
        r = alpha * u32_ref[...] + acc_ref[...]
        o32_ref[...] = _layer_norm(r, g_ref[...], b_ref[...])


def merge_ln(u32, u16, oa, ys, w_gate, b_gate, w_pa, w_pb, w_out, g, b, *, alpha, tm=384, tc=512):
    bsz, t, d = u32.shape
    tm = _pick(t, tm)
    tc = _pick(d, tc)
    nc = d // tc
    grid = (bsz, t // tm, nc)
    row = pl.BlockSpec((None, tm, d), lambda bi, i, c: (bi, i, 0))
    vec = pl.BlockSpec((1, d), lambda bi, i, c: (0, 0))
    col = pl.BlockSpec((d, tc), lambda bi, i, c: (0, c))
    col_hi = pl.BlockSpec((d, tc), lambda bi, i, c: (0, c + nc))
    bias = pl.BlockSpec((1, tc), lambda bi, i, c: (0, c))
    bias_hi = pl.BlockSpec((1, tc), lambda bi, i, c: (0, c + nc))
    return pl.pallas_call(
        functools.partial(_merge_kernel, alpha=alpha),
        grid=grid,
        in_specs=[row, row, row, row, col, col_hi, bias, bias_hi, col, col,
                  pl.BlockSpec((tc, d), lambda bi, i, c: (c, 0)), vec, vec],
        out_specs=row,
        out_shape=jax.ShapeDtypeStruct((bsz, t, d), F32),
        scratch_shapes=[pltpu.VMEM((tm, d), F32)],
        compiler_params=_cparams(("parallel", "parallel", "arbitrary")),
        name="merge_ln",
    )(u32, u16, oa, ys, w_gate, w_gate, b_gate, b_gate, w_pa, w_pb, w_out, g, b)


def _neg_suffix_matrix(tk):
    r = lax.broadcasted_iota(jnp.int32, (tk, tk), 0)
    c = lax.broadcasted_iota(jnp.int32, (tk, tk), 1)
    return jnp.where(r > c, -1.0, 0.0).astype(BF16)


LOG2E = 1.4426950408889634


def _sb_tile(z2, carry, uneg, mask):
    sp = jnp.maximum(z2, 0.0) + jnp.log(1.0 + jnp.exp2(-jnp.abs(z2))) * LOG2E
    if mask is not None:
        sp = jnp.where(mask, sp, 0.0)
    after = jnp.dot(sp.astype(BF16), uneg, preferred_element_type=F32)
    a = jnp.exp2(z2 - sp + after + carry)
    if mask is not None:
        a = jnp.where(mask, a, 0.0)
    return a, carry - jnp.sum(sp, axis=-1, keepdims=True)


def _sb_prompt_kernel(bias_ref, q_ref, k_ref, v_ref, o_ref, acc_ref, z_ref, w_ref, *, tq, tk):
    qi = pl.program_id(2)
    q = q_ref[...]
    bias = bias_ref[pl.program_id(1)]
    uneg = _neg_suffix_matrix(tk)
    ratio = tq // tk

    def put_logits(kblk):
        start = pl.multiple_of(kblk * tk, tk)
        z_ref[kblk % 2] = lax.dot_general(q, k_ref[pl.ds(start, tk), :], (((1,), (1,)), ((), ())),
                                          preferred_element_type=F32) + bias

    def weigh(kblk):
        start = pl.multiple_of(kblk * tk, tk)
        acc_ref[...] += jnp.dot(w_ref[kblk % 2], v_ref[pl.ds(start, tk), :], preferred_element_type=F32)

    def sub_step(kblk, carry, mask, weigh_previous):
        slot = kblk % 2
        z = z_ref[slot]
        put_logits(jnp.maximum(kblk - 1, 0))
        if weigh_previous:
            weigh(kblk + 1)
        a, carry = _sb_tile(z, carry, uneg, mask)
        w_ref[slot] = a.astype(BF16)
        return carry

    acc_ref[...] = jnp.zeros_like(acc_ref)
    qpos = lax.broadcasted_iota(jnp.int32, (tq, tk), 0)
    kloc = lax.broadcasted_iota(jnp.int32, (tq, tk), 1)
    first = qi * ratio
    carry = jnp.zeros((tq, 1), F32)
    put_logits(first + ratio - 1)
    for dblk in reversed(range(ratio)):
        carry = sub_step(first + dblk, carry, (kloc + dblk * tk) < qpos, dblk != ratio - 1)
    lax.fori_loop(0, first, lambda i, c: sub_step(first - 1 - i, c, None, True), carry)
    weigh(0)
    o_ref[...] = acc_ref[...].astype(o_ref.dtype)


def sb_prompt(q, k, v, bias2, *, heads, tq=256, tk=256):
    bsz, t, width = q.shape
    dh = width // heads
    tq = _pick(t, tq)
    tk = _pick(tq, tk)
    return pl.pallas_call(
        functools.partial(_sb_prompt_kernel, tq=tq, tk=tk),
        grid=(bsz, heads, t // tq),
        in_specs=[pl.BlockSpec(memory_space=pltpu.SMEM),
                  pl.BlockSpec((None, tq, dh), lambda bi, h, i: (bi, i, h)),
                  pl.BlockSpec((None, t, dh), lambda bi, h, i: (bi, 0, h)),
                  pl.BlockSpec((None, t, dh), lambda bi, h, i: (bi, 0, h))],
        out_specs=pl.BlockSpec((None, tq, dh), lambda bi, h, i: (bi, i, h)),
        out_shape=jax.ShapeDtypeStruct((bsz, t, width), BF16),
        scratch_shapes=[pltpu.VMEM((tq, dh), F32), pltpu.VMEM((2, tq, tk), F32), pltpu.VMEM((2, tq, tk), BF16)],
        compiler_params=_cparams(("parallel", "parallel", "arbitrary")),
        name="sb_prompt",
    )(bias2, q, k, v)


HEAD_GROUPS = 4
PAGES_PER_CHAIN = 4


def _sb_sample_kernel(pt_ref, qg_ref, bias_ref, knew_ref, vnew_ref, *rest, pages_per_step, heads, dec_seq, page):
    kv_refs = rest[:2 * pages_per_step]
    o_ref, acc_ref, carry_ref = rest[2 * pages_per_step:]
    s = pl.program_id(1)
    hg = heads // HEAD_GROUPS
    ppt = LANES // hg
    rows = qg_ref.shape[1]
    uneg = _neg_suffix_matrix(LANES)
    bias = bias_ref[...]
    r_i = lax.broadcasted_iota(jnp.int32, (HEAD_GROUPS * rows, LANES), 0) % rows
    c_i = lax.broadcasted_iota(jnp.int32, (HEAD_GROUPS * rows, LANES), 1)
    valid = (r_i < hg * dec_seq) & ((c_i % hg) == (r_i // dec_seq))

    def tile_rows(ref, c, tl):
        return ref[pl.ds(c + tl * ppt * heads, LANES, stride=HEAD_GROUPS), :].astype(BF16)

    def do_tiles(tiles, mask):
        carry = carry_ref[...]
        zs = [jnp.concatenate([lax.dot_general(qg_ref[c], tile_rows(k_ref, c, tl), (((1,), (1,)), ((), ())),
                                               preferred_element_type=F32) for c in range(HEAD_GROUPS)], axis=0) + bias
              for k_ref, _, tl in tiles]
        sps = [jnp.where(mask, jnp.maximum(z, 0.0) + jnp.log(1.0 + jnp.exp2(-jnp.abs(z))) * LOG2E, 0.0) for z in zs]
        afters = [jnp.dot(sp.astype(BF16), uneg, preferred_element_type=F32) for sp in sps]
        parts = [None] * HEAD_GROUPS
        for (_, v_ref, tl), z, sp, after in zip(tiles, zs, sps, afters):
            a = jnp.where(mask, jnp.exp2(z - sp + after + carry), 0.0)
            carry = carry - jnp.sum(sp, axis=-1, keepdims=True)
            for c in range(HEAD_GROUPS):
                part = jnp.dot(a[c * rows:(c + 1) * rows].astype(BF16), tile_rows(v_ref, c, tl),
                               preferred_element_type=F32)
                parts[c] = part if parts[c] is None else parts[c] + part
        carry_ref[...] = carry
        for c in range(HEAD_GROUPS):
            acc_ref[c] += parts[c]

    @pl.when(s == 0)
    def _():
        acc_ref[...] = jnp.zeros_like(acc_ref)
        carry_ref[...] = jnp.zeros_like(carry_ref)
        do_tiles([(knew_ref, vnew_ref, 0)], valid & ((c_i // hg) < (r_i % dec_seq)))

    for p in range(0, pages_per_step, PAGES_PER_CHAIN):
        do_tiles([(kv_refs[2 * pp], kv_refs[2 * pp + 1], tl)
                  for pp in range(p, min(p + PAGES_PER_CHAIN, pages_per_step))
                  for tl in reversed(range(page // ppt))], valid)

    @pl.when(s == pl.num_programs(1) - 1)
    def _():
        o_ref[...] = acc_ref[...]


def sb_sample(q, k_new, v_new, cache_k, cache_v, page_table, bias2, *, pages_per_step=8):
    bsz, s, heads, dh = q.shape
    n_phys, page = cache_k.shape[:2]
    n_pages = page_table.shape[1]
    pps = _pick(n_pages, pages_per_step)
    hg = heads // HEAD_GROUPS
    ppt = LANES // hg
    assert heads % HEAD_GROUPS == 0 and LANES % hg == 0 and page % ppt == 0 and s <= ppt
    rows = -(-hg * s // 16) * 16
    qg = q.transpose(0, 2, 1, 3).reshape(bsz, hg, HEAD_GROUPS, s, dh).transpose(0, 2, 1, 3, 4)
    qg = jnp.pad(qg.reshape(bsz, HEAD_GROUPS, hg * s, dh), ((0, 0), (0, 0), (0, rows - hg * s), (0, 0)))
    bg = jnp.broadcast_to(bias2.astype(F32).reshape(hg, HEAD_GROUPS).T[:, :, None], (HEAD_GROUPS, hg, s))
    bias_rows = jnp.pad(bg.reshape(HEAD_GROUPS, hg * s), ((0, 0), (0, rows - hg * s)))
    bias_rows = jnp.broadcast_to(bias_rows.reshape(HEAD_GROUPS * rows, 1), (HEAD_GROUPS * rows, LANES))
    pad = ((0, 0), (0, ppt - s), (0, 0), (0, 0))
    knew = jnp.pad(k_new, pad).reshape(bsz, ppt * heads, dh)
    vnew = jnp.pad(v_new, pad).reshape(bsz, ppt * heads, dh)
    ck = cache_k.reshape(n_phys, page * heads, dh)
    cv = cache_v.reshape(n_phys, page * heads, dh)

    def page_spec(p):
        return pl.BlockSpec((None, page * heads, dh),
                            lambda bi, st, pt: (pt[bi, n_pages - 1 - (st * pps + p)], 0, 0))

    kv_specs, kv_args = [], []
    for p in range(pps):
        kv_specs += [page_spec(p), page_spec(p)]
        kv_args += [ck, cv]
    new_spec = pl.BlockSpec((None, ppt * heads, dh), lambda bi, st, pt: (bi, 0, 0))
    grp_spec = pl.BlockSpec((None, HEAD_GROUPS, rows, dh), lambda bi, st, pt: (bi, 0, 0, 0))
    out = pl.pallas_call(
        functools.partial(_sb_sample_kernel, pages_per_step=pps, heads=heads, dec_seq=s, page=page),
        grid_spec=pltpu.PrefetchScalarGridSpec(
            num_scalar_prefetch=1,
            grid=(bsz, n_pages // pps),
            in_specs=[grp_spec,
                      pl.BlockSpec((HEAD_GROUPS * rows, LANES), lambda bi, st, pt: (0, 0)),
                      new_spec, new_spec] + kv_specs,
            out_specs=grp_spec,
            scratch_shapes=[pltpu.VMEM((HEAD_GROUPS, rows, dh), F32), pltpu.VMEM((HEAD_GROUPS * rows, 1), F32)],
        ),
        out_shape=jax.ShapeDtypeStruct((bsz, HEAD_GROUPS, rows, dh), F32),
        compiler_params=_cparams(("parallel", "arbitrary")),
        name="sb_sample",
    )(page_table, qg, bias_rows, knew, vnew, *kv_args)
    o = out[:, :, :hg * s].reshape(bsz, HEAD_GROUPS, hg, s, dh)
    return o.transpose(0, 3, 2, 1, 4).reshape(bsz, s, heads * dh)


def _split3(x):
    h1 = x.astype(BF16)
    r1 = x - h1.astype(F32)
    h2 = r1.astype(BF16)
    h3 = (r1 - h2.astype(F32)).astype(BF16)
    return h1, h2, h3


def _ssd_kernel(xs_ref, bm_ref, cm_ref, pxs_ref, pbm_ref, pcm_ref, wxs_ref, wbm_ref, wcm_ref,
                bxs_ref, bbm_ref, bcm_ref, dt_ref, dtb_ref, alog_ref, dskip_ref, selx_ref, sell_ref, z_ref, nw_ref,
                h0_ref,
                y_ref, hl_ref, state_ref, winx_ref, winb_ref, winc_ref, *, chunk, t_valid, hpg, hdim):
    c = pl.program_id(2)
    halo = SUBLANES

    @pl.when(c == 0)
    def _():
        state_ref[...] = h0_ref[...]
        winx_ref[0:halo, :] = pxs_ref[...]
        winb_ref[0:halo, :] = pbm_ref[...]
        winc_ref[0:halo, :] = pcm_ref[...]

    valid = (lax.broadcasted_iota(jnp.int32, (chunk, 1), 0) + c * chunk) < t_valid

    def conv_silu(win_ref, cur_ref, w_ref, b_ref):
        cur = cur_ref[...]
        win_ref[halo:halo + chunk, :] = cur
        out = b_ref[...]
        for i in range(CONV_WIDTH):
            lo = halo - (CONV_WIDTH - 1) + i
            out = out + win_ref[lo:lo + chunk, :] * w_ref[i:i + 1, :]
        win_ref[0:halo, :] = cur[chunk - halo:chunk, :]
        return jnp.where(valid, out * jax.nn.sigmoid(out), 0.0)

    xs = conv_silu(winx_ref, xs_ref, wxs_ref, bxs_ref)
    bm = conv_silu(winb_ref, bm_ref, wbm_ref, bbm_ref)
    cm = conv_silu(winc_ref, cm_ref, wcm_ref, bcm_ref)
    bm16 = bm.astype(BF16)
    cm16 = cm.astype(BF16)

    dt = jnp.where(valid, _softplus(dt_ref[...] + dtb_ref[...]), 0.0)
    da = dt * (-jnp.exp(alog_ref[...]))
    ri = lax.broadcasted_iota(jnp.int32, (chunk, chunk), 0)
    ci = lax.broadcasted_iota(jnp.int32, (chunk, chunk), 1)
    causal = ri >= ci
    tril = jnp.where(causal, 1.0, 0.0).astype(BF16)
    acs = sum(jnp.dot(tril, part, preferred_element_type=F32) for part in _split3(da))
    acs_t = acs.T
    acs_parts = _split3(acs)

    def expand(parts, sel_ref):
        return sum(jnp.dot(part, sel_ref[...], preferred_element_type=F32) for part in parts)

    dt_x = expand(_split3(dt)[:2], selx_ref)
    a_x = expand(acs_parts, selx_ref)
    a_l = expand(acs_parts, sell_ref)
    a_end = a_x[chunk - 1:chunk, :]
    xdt = xs * dt_x
    st_all = state_ref[...].reshape(hpg * hdim, state_ref.shape[2])
    cb = lax.dot_general(cm16, bm16, (((1,), (1,)), ((), ())), preferred_element_type=F32)
    y_off = lax.dot_general(cm16, st_all.astype(BF16), (((1,), (1,)), ((), ())),
                            preferred_element_type=F32) * jnp.exp(a_x)
    upd = lax.dot_general((xdt * jnp.exp(a_end - a_x)).astype(BF16), bm16, (((0,), (0,)), ((), ())),
                          preferred_element_type=F32)
    hpb = LANES // hdim
    lane_head = lax.broadcasted_iota(jnp.int32, (1, LANES), 1) // hdim
    y_blocks = []
    for blk in range(hpg // hpb):
        xb = xdt[:, blk * LANES:(blk + 1) * LANES]
        yb = None
        for j in range(hpb):
            r = blk * hpb + j
            decay = jnp.exp(jnp.where(causal, a_l[:, r * LANES:(r + 1) * LANES] - acs_t[r:r + 1, :], -jnp.inf))
            part = jnp.dot((cb * decay).astype(BF16), jnp.where(lane_head == j, xb, 0.0).astype(BF16),
                           preferred_element_type=F32)
            yb = part if yb is None else yb + part
        y_blocks.append(yb)
    for r in range(hpg):
        a_last = acs[chunk - 1:chunk, r:r + 1]
        state_ref[r] = jnp.exp(a_last) * state_ref[r] + upd[r * hdim:(r + 1) * hdim, :]
    y = jnp.concatenate(y_blocks, axis=1) + y_off + xs * dskip_ref[...]
    zz = z_ref[...].astype(F32)
    gated = y * (zz * jax.nn.sigmoid(zz))
    gated = gated * lax.rsqrt(jnp.mean(gated * gated, axis=-1, keepdims=True) + RMS_EPS)
    y_ref[...] = (gated * nw_ref[...]).astype(y_ref.dtype)

    @pl.when(c == pl.num_programs(2) - 1)
    def _():
        hl_ref[...] = state_ref[...]


def ssd_mix(xbc, prev, dt, z, h0, conv_w, conv_b, dt_bias, a_log, d_skip, norm_w, *, t_valid, groups, chunk=128):
    bsz, t, _ = xbc.shape
    heads, hdim, n = h0.shape[1:]
    width = heads * hdim
    hpg = heads // groups
    gw = width // groups
    chunk = _pick(t, chunk)
    nchunk = t // chunk
    assert gw % LANES == 0 and n == LANES
    xoff, boff, coff = 0, width // n, width // n + groups

    def rows(w, off):
        return pl.BlockSpec((None, chunk, w), lambda bi, g, c: (bi, c, g + off) if w == n else (bi, c, g))

    def first(w, off):
        return pl.BlockSpec((None, SUBLANES, w), lambda bi, g, c: (bi, 0, g + off) if w == n else (bi, 0, g))

    def par(r, w, off):
        return pl.BlockSpec((r, w), lambda bi, g, c: (0, g + off) if w == n else (0, g))

    lane = pl.BlockSpec((1, LANES), lambda bi, g, c: (0, g))
    head_of = jnp.arange(LANES)[:, None]
    selx = (head_of == jnp.arange(gw)[None, :] // hdim).astype(BF16)
    sell = (head_of == jnp.arange(hpg * LANES)[None, :] // LANES).astype(BF16)
    dskip_x = jnp.repeat(d_skip.astype(F32), hdim).reshape(1, width)
    st_spec = pl.BlockSpec((None, hpg, hdim, n), lambda bi, g, c: (bi, g, 0, 0))
    return pl.pallas_call(
        functools.partial(_ssd_kernel, chunk=chunk, t_valid=t_valid, hpg=hpg, hdim=hdim),
        grid=(bsz, groups, nchunk),
        in_specs=[rows(gw, xoff), rows(n, boff), rows(n, coff),
                  first(gw, xoff), first(n, boff), first(n, coff),
                  par(CONV_WIDTH, gw, xoff), par(CONV_WIDTH, n, boff), par(CONV_WIDTH, n, coff),
                  par(1, gw, xoff), par(1, n, boff), par(1, n, coff),
                  pl.BlockSpec((None, chunk, LANES), lambda bi, g, c: (bi, c, g)),
                  lane, lane,
                  pl.BlockSpec((1, gw), lambda bi, g, c: (0, g)),
                  pl.BlockSpec((LANES, gw), lambda bi, g, c: (0, 0)),
                  pl.BlockSpec((LANES, hpg * LANES), lambda bi, g, c: (0, 0)),
                  pl.BlockSpec((None, chunk, gw), lambda bi, g, c: (bi, c, g)),
                  pl.BlockSpec((1, gw), lambda bi, g, c: (0, g)),
                  st_spec],
        out_specs=[pl.BlockSpec((None, chunk, gw), lambda bi, g, c: (bi, c, g)), st_spec],
        out_shape=[jax.ShapeDtypeStruct((bsz, t, width), BF16), jax.ShapeDtypeStruct((bsz, heads, hdim, n), F32)],
        scratch_shapes=[pltpu.VMEM((hpg, hdim, n), F32),
                        pltpu.VMEM((chunk + SUBLANES, gw), F32),
                        pltpu.VMEM((chunk + SUBLANES, n), F32),
                        pltpu.VMEM((chunk + SUBLANES, n), F32)],
        compiler_params=_cparams(("parallel", "parallel", "arbitrary")),
        name="ssd_mix",
    )(xbc, xbc, xbc, prev, prev, prev, conv_w, conv_w, conv_w, conv_b, conv_b, conv_b,
      dt, dt_bias, a_log, dskip_x, selx, sell, z, norm_w, h0)


def _group_lanes(v, groups):
    hpg = v.shape[0] // groups
    return jnp.pad(v.astype(F32).reshape(groups, hpg), ((0, 0), (0, LANES - hpg))).reshape(1, groups * LANES)


def kernel(x_prompt, x_sample, cache_k, cache_v, state_conv, state_ssm, page_table, meta_tokens,
           ln1_g, ln1_b, ffn1_w1, ffn1_w3, ffn1_w2, w_in, sb_bias, conv_w, conv_b, dt_bias, a_log, d_skip,
           ssm_norm_w, w_gate, b_gate, w_pa, w_pb, w_out, ln2_g, ln2_b,
           ffn2_w1, ffn2_w3, ffn2_w2, ln3_g, ln3_b):
    depth = w_in.shape[0]
    assert depth == 1
    bsz, seq, d = x_prompt.shape
    dec_b, dec_s, _ = x_sample.shape
    n_meta = meta_tokens.shape[0]
    sb_heads, sb_dh = cache_k.shape[3:]
    sb_width = sb_heads * sb_dh
    page = cache_k.shape[2]
    ssm_heads, ssm_hdim, ssm_n = state_ssm.shape[2:]
    ssm_width = ssm_heads * ssm_hdim
    conv_dim = conv_w.shape[2]
    groups = (conv_dim - ssm_width) // (2 * ssm_n)
    hpg = ssm_heads // groups
    alpha = (2.0 * depth) ** 0.25
    scale = sb_dh ** -0.5 * LOG2E
    t_real = seq + n_meta
    row_tile = 256
    t_pad = -(-t_real // row_tile) * row_tile
    chunk = 128

    l = 0
    bf = lambda w: w.astype(BF16)
    vec = lambda v: v.astype(F32).reshape(1, -1)
    w1a, w3a, w2a = bf(ffn1_w1[l]), bf(ffn1_w3[l]), bf(ffn1_w2[l])
    w1b, w3b, w2b = bf(ffn2_w1[l]), bf(ffn2_w3[l]), bf(ffn2_w2[l])
    cuts = [0, sb_width, 2 * sb_width, 3 * sb_width, 3 * sb_width + ssm_width, 3 * sb_width + ssm_width + conv_dim]
    wq, wk, wv, wz, wx = (bf(w_in[l][:, cuts[i]:cuts[i + 1]]) for i in range(5))
    wdt = w_in[l][:, cuts[5]:].reshape(d, groups, hpg)
    wdt = bf(jnp.pad(wdt, ((0, 0), (0, 0), (0, LANES - hpg))).reshape(d, groups * LANES))
    wg, wpa, wpb, wo = bf(w_gate[l]), bf(w_pa[l]), bf(w_pb[l]), bf(w_out[l])
    dtb_l, alog_l = (_group_lanes(v[l], groups) for v in (dt_bias, a_log))
    cw, cb = conv_w[l].astype(F32), vec(conv_b[l])
    bias = sb_bias[l].astype(F32) * LOG2E

    def pre(h):
        return ffn_ln(h, w1a, w3a, w2a, vec(ln1_g[l]), vec(ln1_b[l]), [F32, BF16], alpha=alpha)

    def project(u16, t_out):
        q, = mm(u16, wq, [(BF16, u16.shape[1])], scale=scale)
        k32, k16 = mm(u16, wk, [(F32, t_out), (BF16, u16.shape[1])])
        v32, v16 = mm(u16, wv, [(F32, t_out), (BF16, u16.shape[1])])
        z16, = mm(u16, wz, [(BF16, u16.shape[1])])
        xbc, = mm(u16, wx, [(F32, u16.shape[1])])
        dt, = mm(u16, wdt, [(F32, u16.shape[1])])
        return q, k32, k16, v32, v16, z16, xbc, dt

    def post(u32, u16, o_att, y_ssm):
        h32 = merge_ln(u32, u16, o_att, y_ssm, wg, vec(b_gate[l]), wpa, wpb, wo,
                       vec(ln2_g[l]), vec(ln2_b[l]), alpha=alpha)
        out, = ffn_ln(h32, w1b, w3b, w2b, vec(ln3_g[l]), vec(ln3_b[l]), [F32], alpha=alpha, tf=512)
        return out

    def ssm(xbc, prev, dt, z16, h0, t_valid):
        return ssd_mix(xbc, prev, dt, z16, h0, cw, cb, dtb_l, alog_l, d_skip[l], vec(ssm_norm_w[l]),
                       t_valid=t_valid, groups=groups, chunk=chunk)

    meta = jnp.broadcast_to(meta_tokens.astype(x_prompt.dtype)[None], (bsz, n_meta, d))
    hp = jnp.concatenate([meta, x_prompt, jnp.zeros((bsz, t_pad - t_real, d), x_prompt.dtype)], axis=1)
    u32, u16 = pre(hp)
    q, k32, k16, v32, v16, z16, xbc, dt = project(u16, t_real)
    o_att = sb_prompt(q, k16, v16, bias, heads=sb_heads)
    y_ssm, h_last_p = ssm(xbc, jnp.zeros((bsz, SUBLANES, conv_dim), F32), dt, z16,
                          jnp.zeros((bsz, ssm_heads, ssm_hdim, ssm_n), F32), t_real)
    y_prompt = post(u32, u16, o_att, y_ssm)[:, n_meta:t_real]
    k_prompt = k32.reshape(1, bsz, t_real, sb_heads, sb_dh)
    v_prompt = v32.reshape(1, bsz, t_real, sb_heads, sb_dh)
    conv_prompt = xbc[:, t_real - (CONV_WIDTH - 1):t_real][None]
    ssm_prompt = h_last_p[None]

    rows = dec_b * dec_s
    u32, u16 = pre(x_sample.reshape(1, rows, d))
    q, k32, _, v32, _, z16, xbc, dt = project(u16, rows)
    k_new = k32.reshape(dec_b, dec_s, sb_heads, sb_dh)
    v_new = v32.reshape(dec_b, dec_s, sb_heads, sb_dh)
    o_att = sb_sample(q.reshape(dec_b, dec_s, sb_heads, sb_dh), k_new, v_new, cache_k[l], cache_v[l], page_table, bias)
    per_seq = lambda a: jnp.pad(a.reshape(dec_b, dec_s, a.shape[-1]), ((0, 0), (0, chunk - dec_s), (0, 0)))
    xbc_s = xbc.reshape(dec_b, dec_s, conv_dim)
    prev = jnp.pad(state_conv[l].astype(F32), ((0, 0), (SUBLANES - (CONV_WIDTH - 1), 0), (0, 0)))
    y_ssm, h_last_s = ssm(per_seq(xbc), prev, per_seq(dt), per_seq(z16), state_ssm[l].astype(F32), dec_s)
    y_ssm = y_ssm[:, :dec_s].reshape(1, rows, ssm_width)
    y_sample = post(u32, u16, o_att.astype(BF16).reshape(1, rows, sb_width), y_ssm).reshape(dec_b, dec_s, d)
    tail = jnp.concatenate([state_conv[l].astype(F32), xbc_s], axis=1)
    conv_sample = tail[:, tail.shape[1] - (CONV_WIDTH - 1):][None]

    return (y_prompt, y_sample, k_prompt, v_prompt, conv_prompt, ssm_prompt,
            k_new[None], v_new[None], conv_sample, h_last_s[None])
```
